```python
import math, functools
import jax, jax.numpy as jnp
from jax import lax
import numpy as np

D_MODEL = 2048
BATCH = 4
SEQ = 2048
DEPTH = 2
DEC_BATCH = 128
DEC_SEQ = 4
PAST_LEN = 2048
PAGE_SIZE = 128

SB_HEADS = 4
SB_DH = 128
SB_W = SB_HEADS * SB_DH
DF_HEADS = 4
DF_DH = 128
DF_QK_W = DF_HEADS * 2 * DF_DH
DF_V_W = DF_HEADS * 2 * DF_DH
CV_WIDTH = 512
CONV_K = 3
D_FF = 4 * D_MODEL
Q_BLOCK = 128
LN_EPS = 1e-5
RMS_EPS = 1e-6
DN_ALPHA = (2 * DEPTH) ** 0.25
DN_BETA = (8 * DEPTH) ** -0.25
D_IN = 3 * SB_W + 2 * DF_QK_W + DF_V_W + 3 * CV_WIDTH + 3 * D_MODEL

kernel_name = "hybrid_stickbreak_diffattn_shortconv_decode_step"


def _split_offsets():
    sizes = [SB_W, SB_W, SB_W, DF_QK_W, DF_QK_W, DF_V_W,
             CV_WIDTH, CV_WIDTH, CV_WIDTH, D_MODEL, D_MODEL, D_MODEL]
    return [int(v) for v in np.cumsum(sizes)[:-1]]


def _layernorm(x, g, b):
    xf = x.astype(jnp.float32)
    mu = jnp.mean(xf, -1, keepdims=True)
    var = jnp.mean(jnp.square(xf - mu), -1, keepdims=True)
    return ((xf - mu) * lax.rsqrt(var + LN_EPS) * g.astype(jnp.float32)
            + b.astype(jnp.float32)).astype(x.dtype)


def _sweep_queries(fn, q, q_pos):
    b, tq = q.shape[0], q.shape[1]
    if tq <= Q_BLOCK or tq % Q_BLOCK:
        return fn(q, q_pos)
    nb = tq // Q_BLOCK
    qb = jnp.moveaxis(q.reshape(b, nb, Q_BLOCK, *q.shape[2:]), 1, 0)
    ob = lax.map(lambda a: fn(a[0], a[1]), (qb, q_pos.reshape(nb, Q_BLOCK)))
    return jnp.moveaxis(ob, 0, 1).reshape(b, tq, *ob.shape[3:])


def _stick_breaking_block(q, q_pos, k, v, k_pos):
    z = jnp.einsum('bqhd,bkhd->bhqk', q, k).astype(jnp.float32) * (SB_DH ** -0.5)
    valid = k_pos[None, :] < q_pos[:, None]
    log_keep = jnp.where(valid, jax.nn.log_sigmoid(-z), 0.0)
    after = lax.cumsum(log_keep, axis=3, reverse=True) - log_keep
    log_a = jax.nn.log_sigmoid(z) + after
    a = jnp.where(valid, jnp.exp(log_a), 0.0)
    return jnp.einsum('bhqk,bkhd->bqhd', a.astype(v.dtype), v)


def _diff_attn_block(q, q_pos, k, v, k_pos, lam):
    s = jnp.einsum('bqhcd,bkhcd->bchqk', q, k).astype(jnp.float32) * (DF_DH ** -0.5)
    dist = (q_pos[:, None] - k_pos[None, :]).astype(jnp.float32)
    slopes = 2.0 ** (-8.0 * jnp.arange(1, DF_HEADS + 1, dtype=jnp.float32) / DF_HEADS)
    s = s - slopes[:, None, None] * dist
    s = jnp.where(dist >= 0, s, -jnp.inf)
    p = jax.nn.softmax(s, axis=-1)
    w = p[:, 0] - lam * p[:, 1]
    return jnp.einsum('bhqk,bkhe->bqhe', w.astype(v.dtype), v)


def _layer(x, past, lp, lam_init):
    (w_in, conv_w, df_lambda, df_norm_g, w_br_sb, w_br_df, w_br_cv, w_o,
     ln1_g, ln1_b, w_up, w_down, ln2_g, ln2_b) = lp
    b, t, _ = x.shape
    proj = jnp.einsum('btd,de->bte', x, w_in)
    (sb_q, sb_k, sb_v, df_q, df_k, df_v, cv_b, cv_c, cv_h,
     g_sb, g_df, g_cv) = jnp.split(proj, _split_offsets(), axis=-1)
    sb_q = sb_q.reshape(b, t, SB_HEADS, SB_DH)
    sb_k = sb_k.reshape(b, t, SB_HEADS, SB_DH)
    sb_v = sb_v.reshape(b, t, SB_HEADS, SB_DH)
    df_q = df_q.reshape(b, t, DF_HEADS, 2, DF_DH)
    df_k = df_k.reshape(b, t, DF_HEADS, 2, DF_DH)
    df_v = df_v.reshape(b, t, DF_HEADS, 2 * DF_DH)
    if past is None:
        k_sb, v_sb, k_df, v_df = sb_k, sb_v, df_k, df_v
        hist = jnp.zeros((b, CONV_K - 1, CV_WIDTH), x.dtype)
    else:
        pk_sb, pv_sb, pk_df, pv_df, hist = past
        k_sb = jnp.concatenate([pk_sb.astype(sb_k.dtype), sb_k], axis=1)
        v_sb = jnp.concatenate([pv_sb.astype(sb_v.dtype), sb_v], axis=1)
        k_df = jnp.concatenate([pk_df.astype(df_k.dtype), df_k], axis=1)
        v_df = jnp.concatenate([pv_df.astype(df_v.dtype), df_v], axis=1)
        hist = hist.astype(x.dtype)
    tk = k_sb.shape[1]
    k_pos = jnp.arange(tk, dtype=jnp.int32)
    q_pos = (tk - t) + jnp.arange(t, dtype=jnp.int32)

    o_sb = _sweep_queries(functools.partial(_stick_breaking_block, k=k_sb, v=v_sb, k_pos=k_pos),
                          sb_q, q_pos).reshape(b, t, SB_W)

    lq1, lk1, lq2, lk2 = [r for r in df_lambda.astype(jnp.float32)]
    lam = jnp.exp(jnp.sum(lq1 * lk1)) - jnp.exp(jnp.sum(lq2 * lk2)) + lam_init
    o_df = _sweep_queries(functools.partial(_diff_attn_block, k=k_df, v=v_df, k_pos=k_pos, lam=lam),
                          df_q, q_pos)
    of = o_df.astype(jnp.float32)
    of = of * lax.rsqrt(jnp.mean(jnp.square(of), -1, keepdims=True) + RMS_EPS)
    o_df = (of * df_norm_g.astype(jnp.float32) * (1.0 - lam_init)).astype(x.dtype).reshape(b, t, DF_V_W)

    u = cv_c * cv_h
    ext = jnp.concatenate([hist, u], axis=1)
    conv = sum(conv_w[j] * ext[:, j:j + t] for j in range(CONV_K))
    o_cv = cv_b * conv
    new_hist = ext[:, ext.shape[1] - (CONV_K - 1):]

    merged = (jax.nn.sigmoid(g_sb) * jnp.einsum('btc,cd->btd', o_sb, w_br_sb)
              + jax.nn.sigmoid(g_df) * jnp.einsum('btc,cd->btd', o_df, w_br_df)
              + jax.nn.sigmoid(g_cv) * jnp.einsum('btc,cd->btd', o_cv, w_br_cv))
    mix = jnp.einsum('btd,de->bte', merged, w_o)
    x = _layernorm(DN_ALPHA * x + mix, ln1_g, ln1_b)

    h = jnp.square(jax.nn.relu(jnp.einsum('btd,df->btf', x, w_up)))
    x = _layernorm(DN_ALPHA * x + jnp.einsum('btf,fd->btd', h, w_down), ln2_g, ln2_b)
    return x, (sb_k, sb_v, df_k, df_v, new_hist)


def _gather_pages(pool, layer, page_table):
    g = pool[layer, page_table]
    return g.reshape(g.shape[0], g.shape[1] * g.shape[2], *g.shape[3:])


def setup_inputs(seed: int = 0) -> dict:
    key = jax.random.key(seed)
    ks = jax.random.split(key, 24)
    n_pages = PAST_LEN // PAGE_SIZE
    n_phys = (DEC_BATCH * n_pages * 5) // 4
    f32 = jnp.float32
    nrm = lambda k, shape, scale: jax.random.normal(k, shape, f32) * scale
    page_table = jax.random.permutation(ks[7], n_phys)[:DEC_BATCH * n_pages].reshape(
        DEC_BATCH, n_pages).astype(jnp.int32)
    return {
        "x_prompt": nrm(ks[0], (BATCH, SEQ, D_MODEL), 1.0),
        "x_sample": nrm(ks[1], (DEC_BATCH, DEC_SEQ, D_MODEL), 1.0),
        "cache_sb_k": nrm(ks[2], (DEPTH, n_phys, PAGE_SIZE, SB_HEADS, SB_DH), 1.0),
        "cache_sb_v": nrm(ks[3], (DEPTH, n_phys, PAGE_SIZE, SB_HEADS, SB_DH), 1.0),
        "cache_df_k": nrm(ks[4], (DEPTH, n_phys, PAGE_SIZE, DF_HEADS, 2, DF_DH), 1.0),
        "cache_df_v": nrm(ks[5], (DEPTH, n_phys, PAGE_SIZE, DF_HEADS, 2 * DF_DH), 1.0),
        "state_conv": nrm(ks[6], (DEPTH, DEC_BATCH, CONV_K - 1, CV_WIDTH), 1.0),
        "page_table": page_table,
        "w_in": nrm(ks[8], (DEPTH, D_MODEL, D_IN), D_MODEL ** -0.5),
        "conv_w": nrm(ks[9], (DEPTH, CONV_K, CV_WIDTH), CONV_K ** -0.5),
        "df_lambda": nrm(ks[10], (DEPTH, 4, DF_DH), 0.1),
        "df_norm_g": 1.0 + nrm(ks[11], (DEPTH, 2 * DF_DH), 0.02),
        "w_br_sb": nrm(ks[12], (DEPTH, SB_W, D_MODEL), SB_W ** -0.5 * DN_BETA),
        "w_br_df": nrm(ks[13], (DEPTH, DF_V_W, D_MODEL), DF_V_W ** -0.5 * DN_BETA),
        "w_br_cv": nrm(ks[14], (DEPTH, CV_WIDTH, D_MODEL), CV_WIDTH ** -0.5 * DN_BETA),
        "w_o": nrm(ks[15], (DEPTH, D_MODEL, D_MODEL), D_MODEL ** -0.5 * DN_BETA),
        "ln1_g": 1.0 + nrm(ks[16], (DEPTH, D_MODEL), 0.02),
        "ln1_b": nrm(ks[17], (DEPTH, D_MODEL), 0.02),
        "w_up": nrm(ks[18], (DEPTH, D_MODEL, D_FF), D_MODEL ** -0.5),
        "w_down": nrm(ks[19], (DEPTH, D_FF, D_MODEL), D_FF ** -0.5 * DN_BETA),
        "ln2_g": 1.0 + nrm(ks[20], (DEPTH, D_MODEL), 0.02),
        "ln2_b": nrm(ks[21], (DEPTH, D_MODEL), 0.02),
    }


def reference(x_prompt, x_sample, cache_sb_k, cache_sb_v, cache_df_k, cache_df_v, state_conv,
              page_table, w_in, conv_w, df_lambda, df_norm_g, w_br_sb, w_br_df, w_br_cv, w_o,
              ln1_g, ln1_b, w_up, w_down, ln2_g, ln2_b):
    yp, ys = x_prompt, x_sample
    new_p = [[] for _ in range(5)]
    new_s = [[] for _ in range(5)]
    for l in range(DEPTH):
        lp = (w_in[l], conv_w[l], df_lambda[l], df_norm_g[l], w_br_sb[l], w_br_df[l], w_br_cv[l],
              w_o[l], ln1_g[l], ln1_b[l], w_up[l], w_down[l], ln2_g[l], ln2_b[l])
        lam_init = 0.8 - 0.6 * math.exp(-0.3 * l)
        past = (_gather_pages(cache_sb_k, l, page_table), _gather_pages(cache_sb_v, l, page_table),
                _gather_pages(cache_df_k, l, page_table), _gather_pages(cache_df_v, l, page_table),
                state_conv[l])
        yp, st_p = _layer(yp, None, lp, lam_init)
        ys, st_s = _layer(ys, past, lp, lam_init)
        for i in range(5):
            new_p[i].append(st_p[i])
            new_s[i].append(st_s[i])
    p_sb_k, p_sb_v, p_df_k, p_df_v, p_conv = [jnp.stack(a, 0) for a in new_p]
    s_sb_k, s_sb_v, s_df_k, s_df_v, s_conv = [jnp.stack(a, 0) for a in new_s]
    return (yp, ys, p_sb_k, p_sb_v, p_df_k, p_df_v, p_conv,
            s_sb_k, s_sb_v, s_df_k, s_df_v, s_conv)
```

```python
import functools
import math

import jax
import jax.numpy as jnp
from jax import lax
from jax.experimental import pallas as pl
from jax.experimental.pallas import tpu as pltpu

F32 = jnp.float32
BF16 = jnp.bfloat16

LN_EPS = 1e-5
RMS_EPS = 1e-6
LANES = 128
SUBLANES = 8
VMEM_LIMIT = 56 * 1024 * 1024
Q_TILE = 128
QPAD = 8


def _cparams(n_axes):
    return pltpu.CompilerParams(dimension_semantics=("arbitrary",) * n_axes,
                                vmem_limit_bytes=VMEM_LIMIT)


def _dot(a, b):
    return jnp.dot(a, b, preferred_element_type=F32)


def _dot_nt(a, b):
    return lax.dot_general(a, b, (((1,), (1,)), ((), ())), preferred_element_type=F32)


def _log_sigmoid(z):
    return jnp.minimum(z, 0.0) - jnp.log1p(jnp.exp(-jnp.abs(z)))


def _layernorm(y, g, b):
    mu = jnp.mean(y, axis=-1, keepdims=True)
    yc = y - mu
    var = jnp.mean(yc * yc, axis=-1, keepdims=True)
    return yc * lax.rsqrt(var + LN_EPS) * g + b


def _scan_matrix(tk):
    r = jnp.arange(tk)[:, None]
    c = jnp.arange(tk)[None, :]
    upper = (r > c).astype(BF16)
    half = jnp.concatenate([upper, jnp.ones((tk, tk), BF16)], axis=1)
    return jnp.concatenate([half, half], axis=0)


def _mm_kernel(x_ref, w_ref, o_ref):
    o_ref[...] = _dot(x_ref[...], w_ref[...]).astype(o_ref.dtype)


def _matmul(x, w, *, tm, tn, name):
    m, k = x.shape
    n = w.shape[1]
    return pl.pallas_call(
        _mm_kernel,
        grid=(n // tn, m // tm),
        in_specs=[pl.BlockSpec((tm, k), lambda j, i: (i, 0)),
                  pl.BlockSpec((k, tn), lambda j, i: (0, j))],
        out_specs=pl.BlockSpec((tm, tn), lambda j, i: (i, j)),
        out_shape=jax.ShapeDtypeStruct((m, n), F32),
        compiler_params=_cparams(2),
        name=name,
    )(x, w)


def _sb_weights(z, u2, carry, valid):
    tk = z.shape[1]
    ls = _log_sigmoid(z)
    lk = ls - z
    if valid is not None:
        lk = jnp.where(valid, lk, 0.0)
    hi = lk.astype(BF16)
    lo = (lk - hi.astype(F32)).astype(BF16)
    sc = _dot(jnp.concatenate([hi, lo], axis=1), u2)
    a = jnp.exp(ls + (sc[:, :tk] + carry))
    if valid is not None:
        a = jnp.where(valid, a, 0.0)
    return a, carry + sc[:, tk:]


def _sb_prompt_kernel(q_ref, k_ref, v_ref, u2_ref, o_ref, *, scale):
    tq, d = q_ref.shape
    qi = pl.program_id(2)
    q = q_ref[...].astype(BF16)
    u2 = u2_ref[...]

    def block(j, carry, acc, valid):
        off = pl.multiple_of(j * tq, tq)
        kb = k_ref[pl.ds(off, tq), :].astype(BF16)
        vb = v_ref[pl.ds(off, tq), :].astype(BF16)
        a, carry = _sb_weights(_dot_nt(q, kb) * scale, u2, carry, valid)
        return carry, acc + _dot(a.astype(BF16), vb)

    row = lax.broadcasted_iota(jnp.int32, (tq, tq), 0)
    col = lax.broadcasted_iota(jnp.int32, (tq, tq), 1)
    carry, acc = block(qi, jnp.zeros((tq, tq), F32), jnp.zeros((tq, d), F32), col < row)
    carry, acc = lax.fori_loop(
        0, qi, lambda t, c: block(qi - 1 - t, c[0], c[1], None), (carry, acc))
    o_ref[...] = acc.astype(o_ref.dtype)


def _sb_prompt(proj, u2, *, batch, seq, heads, dh, q_col, k_col, v_col):
    nq = seq // Q_TILE
    kern = functools.partial(_sb_prompt_kernel, scale=dh ** -0.5)
    return pl.pallas_call(
        kern,
        grid=(batch, heads, nq),
        in_specs=[
            pl.BlockSpec((Q_TILE, dh), lambda b, h, i: (b * nq + i, q_col // dh + h)),
            pl.BlockSpec((seq, dh), lambda b, h, i: (b, k_col // dh + h)),
            pl.BlockSpec((seq, dh), lambda b, h, i: (b, v_col // dh + h)),
            pl.BlockSpec(u2.shape, lambda b, h, i: (0, 0)),
        ],
        out_specs=pl.BlockSpec((Q_TILE, dh), lambda b, h, i: (b * nq + i, h)),
        out_shape=jax.ShapeDtypeStruct((batch * seq, heads * dh), BF16),
        compiler_params=_cparams(3),
        name="sb_prompt",
    )(proj, proj, proj, u2)


def _alibi_slope(h, heads):
    s = jnp.float32(2.0 ** (-8.0 * heads / heads))
    for i in range(heads - 2, -1, -1):
        s = jnp.where(h == i, jnp.float32(2.0 ** (-8.0 * (i + 1) / heads)), s)
    return s


def _df_lambda(lam_ref, lam_init):
    l = lam_ref[...]
    s1 = jnp.sum(l[0:1] * l[1:2], axis=1, keepdims=True)
    s2 = jnp.sum(l[2:3] * l[3:4], axis=1, keepdims=True)
    return jnp.exp(s1) - jnp.exp(s2) + lam_init


def _df_finish(acc0, l0, acc1, l1, lam, g, lam_init):
    o = acc0 / l0 - lam * (acc1 / l1)
    o = o * lax.rsqrt(jnp.mean(o * o, axis=-1, keepdims=True) + RMS_EPS)
    return o * g * (1.0 - lam_init)


def _df_prompt_kernel(q_ref, k_ref, v_ref, lam_ref, g_ref, o_ref, *, scale, lam_init, heads):
    tq = q_ref.shape[0]
    dh = q_ref.shape[1] // 2
    h = pl.program_id(1)
    qi = pl.program_id(2)
    slope = _alibi_slope(h, heads)
    q = q_ref[...].astype(BF16)
    qs = (q[:, :dh], q[:, dh:])
    row = lax.broadcasted_iota(jnp.int32, (tq, tq), 0)
    col = lax.broadcasted_iota(jnp.int32, (tq, tq), 1)
    rel = (row - col).astype(F32)

    def scores(j, c, kb):
        dist = rel + ((qi - j) * tq).astype(F32)
        return _dot_nt(qs[c], kb[:, c * dh:(c + 1) * dh]) * scale - slope * dist

    off = pl.multiple_of(qi * tq, tq)
    kb = k_ref[pl.ds(off, tq), :].astype(BF16)
    vb = v_ref[pl.ds(off, tq), :].astype(BF16)
    state = []
    for c in range(2):
        s = jnp.where(rel >= 0, scores(qi, c, kb), -jnp.inf)
        m = jnp.max(s, axis=1, keepdims=True)
        p = jnp.exp(s - m)
        state += [m, jnp.sum(p, axis=1, keepdims=True), _dot(p.astype(BF16), vb)]

    def body(t, st):
        j = qi - 1 - t
        o = pl.multiple_of(j * tq, tq)
        kb = k_ref[pl.ds(o, tq), :].astype(BF16)
        vb = v_ref[pl.ds(o, tq), :].astype(BF16)
        out = []
        for c in range(2):
            m, l, acc = st[3 * c:3 * c + 3]
            s = scores(j, c, kb)
            m_new = jnp.maximum(m, jnp.max(s, axis=1, keepdims=True))
            alpha = jnp.exp(m - m_new)
            p = jnp.exp(s - m_new)
            out += [m_new, alpha * l + jnp.sum(p, axis=1, keepdims=True),
                    alpha * acc + _dot(p.astype(BF16), vb)]
        return tuple(out)

    st = lax.fori_loop(0, qi, body, tuple(state))
    lam = _df_lambda(lam_ref, lam_init)
    o_ref[...] = _df_finish(st[2], st[1], st[5], st[4], lam, g_ref[...], lam_init).astype(o_ref.dtype)


def _df_prompt(proj, lam_p, g, *, batch, seq, heads, dh, q_col, k_col, v_col, lam_init):
    nq = seq // Q_TILE
    w = 2 * dh
    kern = functools.partial(_df_prompt_kernel, scale=dh ** -0.5, lam_init=lam_init, heads=heads)
    return pl.pallas_call(
        kern,
        grid=(batch, heads, nq),
        in_specs=[
            pl.BlockSpec((Q_TILE, w), lambda b, h, i: (b * nq + i, q_col // w + h)),
            pl.BlockSpec((seq, w), lambda b, h, i: (b, k_col // w + h)),
            pl.BlockSpec((seq, w), lambda b, h, i: (b, v_col // w + h)),
            pl.BlockSpec(lam_p.shape, lambda b, h, i: (0, 0)),
            pl.BlockSpec(g.shape, lambda b, h, i: (0, 0)),
        ],
        out_specs=pl.BlockSpec((Q_TILE, w), lambda b, h, i: (b * nq + i, h)),
        out_shape=jax.ShapeDtypeStruct((batch * seq, heads * w), BF16),
        compiler_params=_cparams(3),
        name="df_prompt",
    )(proj, proj, proj, lam_p, g)


def _decode_kernel(pt_ref, qkv_ref, sbk_ref, sbv_ref, dfk_ref, dfv_ref, u2_ref, lam_ref, g_ref,
                   osb_ref, odf_ref, sb_carry, sb_acc, df_m, df_l, df_acc,
                   *, sb_heads, sb_dh, df_heads, df_dh, cols, lam_init, past_len):
    del pt_ref
    t = pl.program_id(1)
    n_steps = pl.num_programs(1)
    page = sbk_ref.shape[0]
    sb_q_col, sb_k_col, sb_v_col, df_q_col, df_k_col, df_v_col = cols
    sb_scale = sb_dh ** -0.5
    df_scale = df_dh ** -0.5
    dv = 2 * df_dh

    def qslice(col, width):
        return qkv_ref[:, col:col + width].astype(BF16)

    def pad_rows(x):
        return jnp.concatenate([x, jnp.zeros((page - QPAD, x.shape[1]), x.dtype)], axis=0)

    sb_rows = sb_heads * QPAD
    df_rows = df_heads * 2 * QPAD

    def sb_update(ks, vs, kpos0, first):
        z = jnp.concatenate(
            [_dot_nt(qslice(sb_q_col + h * sb_dh, sb_dh), ks[h]) for h in range(sb_heads)],
            axis=0) * sb_scale
        qpos = past_len + (lax.broadcasted_iota(jnp.int32, z.shape, 0) & (QPAD - 1))
        kpos = kpos0 + lax.broadcasted_iota(jnp.int32, z.shape, 1)
        carry = jnp.zeros(z.shape, F32) if first else sb_carry[...]
        a, carry = _sb_weights(z, u2_ref[...], carry, kpos < qpos)
        a = a.astype(BF16)
        pv = jnp.concatenate(
            [_dot(a[h * QPAD:(h + 1) * QPAD], vs[h]) for h in range(sb_heads)], axis=0)
        sb_carry[...] = carry
        sb_acc[...] = pv if first else sb_acc[...] + pv

    def df_update(ks, vs, kpos0, first):
        s = jnp.concatenate(
            [_dot_nt(qslice(df_q_col + (2 * h + c) * df_dh, df_dh), ks[h][c])
             for h in range(df_heads) for c in range(2)], axis=0) * df_scale
        ridx = lax.broadcasted_iota(jnp.int32, s.shape, 0)
        qpos = past_len + (ridx & (QPAD - 1))
        kpos = kpos0 + lax.broadcasted_iota(jnp.int32, s.shape, 1)
        dist = (qpos - kpos).astype(F32)
        slope = _alibi_slope(ridx // (2 * QPAD), df_heads)
        s = jnp.where(dist >= 0, s - slope * dist, -jnp.inf)
        m_blk = jnp.max(s, axis=1, keepdims=True)
        m_new = m_blk if first else jnp.maximum(df_m[...], m_blk)
        p = jnp.exp(s - m_new)
        psum = jnp.sum(p, axis=1, keepdims=True)
        p = p.astype(BF16)
        pv = jnp.concatenate(
            [_dot(p[h * 2 * QPAD:(h + 1) * 2 * QPAD], vs[h]) for h in range(df_heads)], axis=0)
        if first:
            df_l[...] = psum
            df_acc[...] = pv
        else:
            alpha = jnp.exp(df_m[...] - m_new)
            df_l[...] = alpha * df_l[...] + psum
            df_acc[...] = alpha * df_acc[...] + pv
        df_m[...] = m_new

    @pl.when(t == 0)
    def _new_keys():
        ks = [pad_rows(qslice(sb_k_col + h * sb_dh, sb_dh)) for h in range(sb_heads)]
        vs = [pad_rows(qslice(sb_v_col + h * sb_dh, sb_dh)) for h in range(sb_heads)]
        sb_update(ks, vs, past_len, True)
        kd = [[pad_rows(qslice(df_k_col + (2 * h + c) * df_dh, df_dh)) for c in range(2)]
              for h in range(df_heads)]
        vd = [pad_rows(qslice(df_v_col + h * dv, dv)) for h in range(df_heads)]
        df_update(kd, vd, past_len, True)

    @pl.when(t > 0)
    def _cached_page():
        kpos0 = (n_steps - 1 - t) * page
        ks = [sbk_ref[:, h, :].astype(BF16) for h in range(sb_heads)]
        vs = [sbv_ref[:, h, :].astype(BF16) for h in range(sb_heads)]
        sb_update(ks, vs, kpos0, False)
        kd = [[dfk_ref[:, h, c, :].astype(BF16) for c in range(2)] for h in range(df_heads)]
        vd = [dfv_ref[:, h, :].astype(BF16) for h in range(df_heads)]
        df_update(kd, vd, kpos0, False)

    @pl.when(t == n_steps - 1)
    def _finish():
        for h in range(sb_heads):
            osb_ref[:, h * sb_dh:(h + 1) * sb_dh] = sb_acc[h * QPAD:(h + 1) * QPAD, :].astype(osb_ref.dtype)
        lam = _df_lambda(lam_ref, lam_init)
        g = g_ref[...]
        for h in range(df_heads):
            r0 = h * 2 * QPAD
            o = _df_finish(df_acc[r0:r0 + QPAD, :], df_l[r0:r0 + QPAD, :],
                           df_acc[r0 + QPAD:r0 + 2 * QPAD, :], df_l[r0 + QPAD:r0 + 2 * QPAD, :],
                           lam, g, lam_init)
            odf_ref[:, h * dv:(h + 1) * dv] = o.astype(odf_ref.dtype)


def _decode_attn(qkv_new, page_table, cache_sb_k, cache_sb_v, cache_df_k, cache_df_v, u2, lam_p, g,
                 *, layer, cols, lam_init):
    nseq, _, width = qkv_new.shape
    n_pages = page_table.shape[1]
    _, _, page, sb_heads, sb_dh = cache_sb_k.shape
    df_heads, df_dh = cache_df_k.shape[3], cache_df_k.shape[5]
    dv = 2 * df_dh
    past_len = n_pages * page

    def slot(t):
        return n_pages - jnp.maximum(t, 1)

    def page_map(ndim):
        def index_map(b, t, pt):
            return (layer, pt[b * n_pages + slot(t)]) + (0,) * (ndim - 2)
        return index_map

    kern = functools.partial(_decode_kernel, sb_heads=sb_heads, sb_dh=sb_dh, df_heads=df_heads,
                             df_dh=df_dh, cols=cols, lam_init=lam_init, past_len=past_len)
    grid_spec = pltpu.PrefetchScalarGridSpec(
        num_scalar_prefetch=1,
        grid=(nseq, n_pages + 1),
        in_specs=[
            pl.BlockSpec((None, QPAD, width), lambda b, t, pt: (b, 0, 0)),
            pl.BlockSpec((None, None, page, sb_heads, sb_dh), page_map(5)),
            pl.BlockSpec((None, None, page, sb_heads, sb_dh), page_map(5)),
            pl.BlockSpec((None, None, page, df_heads, 2, df_dh), page_map(6)),
            pl.BlockSpec((None, None, page, df_heads, dv), page_map(5)),
            pl.BlockSpec(u2.shape, lambda b, t, pt: (0, 0)),
            pl.BlockSpec(lam_p.shape, lambda b, t, pt: (0, 0)),
            pl.BlockSpec(g.shape, lambda b, t, pt: (0, 0)),
        ],
        out_specs=[
            pl.BlockSpec((None, QPAD, sb_heads * sb_dh), lambda b, t, pt: (b, 0, 0)),
            pl.BlockSpec((None, QPAD, df_heads * dv), lambda b, t, pt: (b, 0, 0)),
        ],
        scratch_shapes=[
            pltpu.VMEM((sb_heads * QPAD, page), F32),
            pltpu.VMEM((sb_heads * QPAD, sb_dh), F32),
            pltpu.VMEM((df_heads * 2 * QPAD, 1), F32),
            pltpu.VMEM((df_heads * 2 * QPAD, 1), F32),
            pltpu.VMEM((df_heads * 2 * QPAD, dv), F32),
        ],
    )
    return pl.pallas_call(
        kern,
        grid_spec=grid_spec,
        out_shape=[jax.ShapeDtypeStruct((nseq, QPAD, sb_heads * sb_dh), BF16),
                   jax.ShapeDtypeStruct((nseq, QPAD, df_heads * dv), BF16)],
        compiler_params=_cparams(2),
        name="decode_attn",
    )(page_table.reshape(-1), qkv_new, cache_sb_k, cache_sb_v, cache_df_k, cache_df_v, u2, lam_p, g)


def _conv_prompt_kernel(b_ref, c_ref, h_ref, w_ref, o_ref, hist_ref):
    u = c_ref[...] * h_ref[...]
    s = u.shape[0]
    row = lax.broadcasted_iota(jnp.int32, u.shape, 0)
    u1 = jnp.where(row >= 1, pltpu.roll(u, 1, axis=0), 0.0)
    u2 = jnp.where(row >= 2, pltpu.roll(u, 2, axis=0), 0.0)
    w = w_ref[...]
    conv = w[0:1] * u2 + w[1:2] * u1 + w[2:3] * u
    o_ref[...] = (b_ref[...] * conv).astype(o_ref.dtype)
    hist_ref[...] = u[s - 2:s, :]


def _conv_prompt(proj, conv_w, *, batch, seq, width, b_col):
    nc = width // LANES
    cb = b_col // LANES
    return pl.pallas_call(
        _conv_prompt_kernel,
        grid=(batch, nc),
        in_specs=[
            pl.BlockSpec((seq, LANES), lambda b, c: (b, cb + c)),
            pl.BlockSpec((seq, LANES), lambda b, c: (b, cb + nc + c)),
            pl.BlockSpec((seq, LANES), lambda b, c: (b, cb + 2 * nc + c)),
            pl.BlockSpec((conv_w.shape[0], LANES), lambda b, c: (0, c)),
        ],
        out_specs=[pl.BlockSpec((seq, LANES), lambda b, c: (b, c)),
                   pl.BlockSpec((None, 2, LANES), lambda b, c: (b, 0, c))],
        out_shape=[jax.ShapeDtypeStruct((batch * seq, width), BF16),
                   jax.ShapeDtypeStruct((batch, 2, width), F32)],
        compiler_params=_cparams(2),
        name="conv_prompt",
    )(proj, proj, proj, conv_w)


def _conv_sample_kernel(b_ref, c_ref, h_ref, hist_ref, w_ref, o_ref, nh_ref, *, width, steps):
    u = c_ref[...] * h_ref[...]
    ext = jnp.concatenate([hist_ref[...], u], axis=1)
    w = w_ref[...]
    b = b_ref[...]
    for q in range(steps):
        conv = (w[0:1] * ext[:, q * width:(q + 1) * width]
                + w[1:2] * ext[:, (q + 1) * width:(q + 2) * width]
                + w[2:3] * ext[:, (q + 2) * width:(q + 3) * width])
        o_ref[:, q * width:(q + 1) * width] = (b[:, q * width:(q + 1) * width] * conv).astype(o_ref.dtype)
    nh_ref[...] = ext[:, steps * width:(steps + 2) * width]


def _conv_sample(cv_b, cv_c, cv_h, hist, conv_w, *, width, steps):
    nseq = cv_b.shape[0]
    kern = functools.partial(_conv_sample_kernel, width=width, steps=steps)
    full = lambda a: pl.BlockSpec(a.shape, lambda i: (0,) * a.ndim)
    return pl.pallas_call(
        kern,
        grid=(1,),
        in_specs=[full(cv_b), full(cv_c), full(cv_h), full(hist), full(conv_w)],
        out_specs=[pl.BlockSpec((nseq, steps * width), lambda i: (0, 0)),
                   pl.BlockSpec((nseq, 2 * width), lambda i: (0, 0))],
        out_shape=[jax.ShapeDtypeStruct((nseq, steps * width), BF16),
                   jax.ShapeDtypeStruct((nseq, 2 * width), F32)],
        compiler_params=_cparams(1),
        name="conv_sample",
    )(cv_b, cv_c, cv_h, hist, conv_w)


def _merge_kernel(osb_ref, odf_ref, ocv_ref, gsb_ref, gdf_ref, gcv_ref, wsb_ref, wdf_ref, wcv_ref, o_ref):
    m = (jax.nn.sigmoid(gsb_ref[...]) * _dot(osb_ref[...], wsb_ref[...])
         + jax.nn.sigmoid(gdf_ref[...]) * _dot(odf_ref[...], wdf_ref[...])
         + jax.nn.sigmoid(gcv_ref[...]) * _dot(ocv_ref[...], wcv_ref[...]))
    o_ref[...] = m.astype(o_ref.dtype)


def _merge(o_sb, o_df, o_cv, proj, w_sb, w_df, w_cv, *, gate_col, tm):
    t = o_sb.shape[0]
    d = w_sb.shape[1]
    gb = gate_col // d
    row = lambda w: pl.BlockSpec((tm, w), lambda i: (i, 0))
    const = lambda a: pl.BlockSpec(a.shape, lambda i: (0, 0))
    return pl.pallas_call(
        _merge_kernel,
        grid=(t // tm,),
        in_specs=[row(o_sb.shape[1]), row(o_df.shape[1]), row(o_cv.shape[1]),
                  pl.BlockSpec((tm, d), lambda i: (i, gb)),
                  pl.BlockSpec((tm, d), lambda i: (i, gb + 1)),
                  pl.BlockSpec((tm, d), lambda i: (i, gb + 2)),
                  const(w_sb), const(w_df), const(w_cv)],
        out_specs=pl.BlockSpec((tm, d), lambda i: (i, 0)),
        out_shape=jax.ShapeDtypeStruct((t, d), BF16),
        compiler_params=_cparams(1),
        name="gated_merge",
    )(o_sb, o_df, o_cv, proj, proj, proj, w_sb, w_df, w_cv)


def _oproj_ln_kernel(m_ref, wo_ref, x_ref, g_ref, b_ref, xo_ref, xb_ref, *, alpha):
    y = alpha * x_ref[...] + _dot(m_ref[...], wo_ref[...])
    out = _layernorm(y, g_ref[...], b_ref[...])
    xo_ref[...] = out
    xb_ref[...] = out.astype(xb_ref.dtype)


def _oproj_ln(merged, w_o, x, g, b, *, alpha, tm):
    t, d = x.shape
    row = pl.BlockSpec((tm, d), lambda i: (i, 0))
    const = lambda a: pl.BlockSpec(a.shape, lambda i: (0, 0))
    return pl.pallas_call(
        functools.partial(_oproj_ln_kernel, alpha=alpha),
        grid=(t // tm,),
        in_specs=[row, const(w_o), row, const(g), const(b)],
        out_specs=[row, row],
        out_shape=[jax.ShapeDtypeStruct((t, d), F32), jax.ShapeDtypeStruct((t, d), BF16)],
        compiler_params=_cparams(1),
        name="oproj_ln",
    )(merged, w_o, x, g, b)


def _mlp_kernel(xb_ref, wu_ref, wd_ref, x_ref, g_ref, b_ref, xo_ref, xbo_ref, acc_ref, *, alpha):
    f = pl.program_id(1)
    h = jnp.maximum(_dot(xb_ref[...], wu_ref[...]), 0.0)
    part = _dot((h * h).astype(BF16), wd_ref[...])

    @pl.when(f == 0)
    def _():
        acc_ref[...] = part

    @pl.when(f > 0)
    def _():
        acc_ref[...] += part

    @pl.when(f == pl.num_programs(1) - 1)
    def _():
        out = _layernorm(alpha * x_ref[...] + acc_ref[...], g_ref[...], b_ref[...])
        xo_ref[...] = out
        xbo_ref[...] = out.astype(xbo_ref.dtype)


def _mlp_ln(xb, w_up, w_down, x, g, b, *, alpha, tm, tf):
    t, d = x.shape
    ff = w_up.shape[1]
    row = pl.BlockSpec((tm, d), lambda i, f: (i, 0))
    const = lambda a: pl.BlockSpec(a.shape, lambda i, f: (0, 0))
    return pl.pallas_call(
        functools.partial(_mlp_kernel, alpha=alpha),
        grid=(t // tm, ff // tf),
        in_specs=[row,
                  pl.BlockSpec((d, tf), lambda i, f: (0, f)),
                  pl.BlockSpec((tf, d), lambda i, f: (f, 0)),
                  row, const(g), const(b)],
        out_specs=[row, row],
        out_shape=[jax.ShapeDtypeStruct((t, d), F32), jax.ShapeDtypeStruct((t, d), BF16)],
        scratch_shapes=[pltpu.VMEM((tm, d), F32)],
        compiler_params=_cparams(2),
        name="mlp_ln",
    )(xb, w_up, w_down, x, g, b)


def _tile(n, pref):
    t = min(n, pref)
    while n % t:
        t //= 2
    return t


def kernel(x_prompt, x_sample, cache_sb_k, cache_sb_v, cache_df_k, cache_df_v, state_conv, page_table,
           w_in, conv_w, df_lambda, df_norm_g, w_br_sb, w_br_df, w_br_cv, w_o, ln1_g, ln1_b,
           w_up, w_down, ln2_g, ln2_b):
    batch, seq, d = x_prompt.shape
    nseq, steps, _ = x_sample.shape
    depth = w_in.shape[0]
    sb_heads, sb_dh = cache_sb_k.shape[3], cache_sb_k.shape[4]
    df_heads, df_dh = cache_df_k.shape[3], cache_df_k.shape[5]
    cv_w = state_conv.shape[3]
    sb_w = sb_heads * sb_dh
    df_w = df_heads * 2 * df_dh
    tp = batch * seq
    ts = nseq * steps
    alpha = (2 * depth) ** 0.25

    sb_q_col, sb_k_col, sb_v_col = 0, sb_w, 2 * sb_w
    df_q_col = 3 * sb_w
    df_k_col, df_v_col = df_q_col + df_w, df_q_col + 2 * df_w
    cv_col = df_q_col + 3 * df_w
    gate_col = cv_col + 3 * cv_w
    cols = (sb_q_col, sb_k_col, sb_v_col, df_q_col, df_k_col, df_v_col)

    u2 = _scan_matrix(Q_TILE)
    x = jnp.concatenate([x_prompt.reshape(tp, d), x_sample.reshape(ts, d)], axis=0)
    xb = x.astype(BF16)
    tm = _tile(tp + ts, 512)

    new_p = [[] for _ in range(5)]
    new_s = [[] for _ in range(5)]
    for l in range(depth):
        lam_init = 0.8 - 0.6 * math.exp(-0.3 * l)
        g_df = df_norm_g[l].reshape(1, -1)
        proj = _matmul(xb, w_in[l].astype(BF16), tm=tm, tn=_tile(w_in.shape[2], 1024), name="in_proj")

        o_sb_p = _sb_prompt(proj, u2, batch=batch, seq=seq, heads=sb_heads, dh=sb_dh,
                            q_col=sb_q_col, k_col=sb_k_col, v_col=sb_v_col)
        o_df_p = _df_prompt(proj, df_lambda[l], g_df, batch=batch, seq=seq, heads=df_heads, dh=df_dh,
                            q_col=df_q_col, k_col=df_k_col, v_col=df_v_col, lam_init=lam_init)
        o_cv_p, hist_p = _conv_prompt(proj, conv_w[l], batch=batch, seq=seq, width=cv_w, b_col=cv_col)

        proj_s = proj[tp:]
        qkv_new = jnp.pad(proj_s[:, :cv_col].reshape(nseq, steps, cv_col),
                          ((0, 0), (0, QPAD - steps), (0, 0)))
        o_sb_s, o_df_s = _decode_attn(qkv_new, page_table, cache_sb_k, cache_sb_v, cache_df_k,
                                      cache_df_v, u2, df_lambda[l], g_df,
                                      layer=l, cols=cols, lam_init=lam_init)
        o_sb_s = o_sb_s[:, :steps].reshape(ts, sb_w)
        o_df_s = o_df_s[:, :steps].reshape(ts, df_w)
        cvs = [proj_s[:, cv_col + i * cv_w:cv_col + (i + 1) * cv_w].reshape(nseq, steps * cv_w)
               for i in range(3)]
        o_cv_s, hist_s = _conv_sample(cvs[0], cvs[1], cvs[2], state_conv[l].reshape(nseq, 2 * cv_w),
                                      conv_w[l], width=cv_w, steps=steps)

        o_sb = jnp.concatenate([o_sb_p, o_sb_s], axis=0)
        o_df = jnp.concatenate([o_df_p, o_df_s], axis=0)
        o_cv = jnp.concatenate([o_cv_p, o_cv_s.reshape(ts, cv_w)], axis=0)
        merged = _merge(o_sb, o_df, o_cv, proj, w_br_sb[l].astype(BF16), w_br_df[l].astype(BF16),
                        w_br_cv[l].astype(BF16), gate_col=gate_col, tm=_tile(tp + ts, 256))
        x, xb = _oproj_ln(merged, w_o[l].astype(BF16), x, ln1_g[l].reshape(1, d), ln1_b[l].reshape(1, d),
                          alpha=alpha, tm=tm)
        x, xb = _mlp_ln(xb, w_up[l].astype(BF16), w_down[l].astype(BF16), x,
                        ln2_g[l].reshape(1, d), ln2_b[l].reshape(1, d),
                        alpha=alpha, tm=tm, tf=_tile(w_up.shape[2], 512))

        kv = (proj[:, sb_k_col:sb_k_col + sb_w], proj[:, sb_v_col:sb_v_col + sb_w],
              proj[:, df_k_col:df_k_col + df_w], proj[:, df_v_col:df_v_col + df_w])
        shapes = ((sb_heads, sb_dh), (sb_heads, sb_dh), (df_heads, 2, df_dh), (df_heads, 2 * df_dh))
        for i in range(4):
            new_p[i].append(kv[i][:tp].reshape(batch, seq, *shapes[i]))
            new_s[i].append(kv[i][tp:].reshape(nseq, steps, *shapes[i]))
        new_p[4].append(hist_p)
        new_s[4].append(hist_s.reshape(nseq, 2, cv_w))

    outs_p = [jnp.stack(a, 0) for a in new_p]
    outs_s = [jnp.stack(a, 0) for a in new_s]
    return (x[:tp].reshape(batch, seq, d), x[tp:].reshape(nseq, steps, d), *outs_p, *outs_s)
```

```python
import functools
import math

import jax
import jax.numpy as jnp
from jax import lax
from jax.experimental import pallas as pl
from jax.experimental.pallas import tpu as pltpu

F32 = jnp.float32
BF16 = jnp.bfloat16

LN_EPS = 1e-5
RMS_EPS = 1e-6
LANES = 128
SUBLANES = 8
VMEM_LIMIT = 56 * 1024 * 1024
Q_TILE = 256
SCAN_W = 128
QPAD = 8
CHUNK_PAGES = 2
N_SLOTS = 2


def _cparams(n_axes):
    return pltpu.CompilerParams(dimension_semantics=("arbitrary",) * n_axes,
                                vmem_limit_bytes=VMEM_LIMIT)


def _dot(a, b):
    return jnp.dot(a, b, preferred_element_type=F32)


def _dot_nt(a, b):
    return lax.dot_general(a, b, (((1,), (1,)), ((), ())), preferred_element_type=F32)


def _log_sigmoid(z):
    return jnp.minimum(z, 0.0) - jnp.log1p(jnp.exp(-jnp.abs(z)))


def _layernorm(y, g, b):
    mu = jnp.mean(y, axis=-1, keepdims=True)
    yc = y - mu
    var = jnp.mean(yc * yc, axis=-1, keepdims=True)
    return yc * lax.rsqrt(var + LN_EPS) * g + b


def _scan_matrix(tk):
    r = jnp.arange(tk)[:, None]
    c = jnp.arange(tk)[None, :]
    upper = (r > c).astype(BF16)
    half = jnp.concatenate([upper, jnp.ones((tk, tk), BF16)], axis=1)
    return jnp.concatenate([half, half], axis=0)


def _mm_kernel(x_ref, w_ref, o_ref):
    o_ref[...] = _dot(x_ref[...], w_ref[...]).astype(o_ref.dtype)


def _matmul(x, w, *, tm, tn, name):
    m, k = x.shape
    n = w.shape[1]
    return pl.pallas_call(
        _mm_kernel,
        grid=(n // tn, m // tm),
        in_specs=[pl.BlockSpec((tm, k), lambda j, i: (i, 0)),
                  pl.BlockSpec((k, tn), lambda j, i: (0, j))],
        out_specs=pl.BlockSpec((tm, tn), lambda j, i: (i, j)),
        out_shape=jax.ShapeDtypeStruct((m, n), F32),
        compiler_params=_cparams(2),
        name=name,
    )(x, w)


def _sb_weights(z, u2, carry, valid):
    w = u2.shape[0] // 2
    nseg = z.shape[1] // w
    ls = _log_sigmoid(z)
    lk = ls - z
    if valid is not None:
        lk = jnp.where(valid, lk, 0.0)
    hi = lk.astype(BF16)
    lo = (lk - hi.astype(F32)).astype(BF16)
    after = [None] * nseg
    for s in reversed(range(nseg)):
        seg = slice(s * w, (s + 1) * w)
        sc = _dot(jnp.concatenate([hi[:, seg], lo[:, seg]], axis=1), u2)
        after[s] = sc[:, :w] + carry
        carry = carry + sc[:, w:]
    a = jnp.exp(ls + (after[0] if nseg == 1 else jnp.concatenate(after, axis=1)))
    if valid is not None:
        a = jnp.where(valid, a, 0.0)
    return a, carry


def _sb_prompt_kernel(q_ref, k_ref, v_ref, u2_ref, o_ref, *, scale):
    tq, d = q_ref.shape
    qi = pl.program_id(2)
    q = q_ref[...].astype(BF16)
    u2 = u2_ref[...]

    def block(j, carry, acc, valid):
        off = pl.multiple_of(j * tq, tq)
        kb = k_ref[pl.ds(off, tq), :].astype(BF16)
        vb = v_ref[pl.ds(off, tq), :].astype(BF16)
        a, carry = _sb_weights(_dot_nt(q, kb) * scale, u2, carry, valid)
        return carry, acc + _dot(a.astype(BF16), vb)

    row = lax.broadcasted_iota(jnp.int32, (tq, tq), 0)
    col = lax.broadcasted_iota(jnp.int32, (tq, tq), 1)
    carry, acc = block(qi, jnp.zeros((tq, SCAN_W), F32), jnp.zeros((tq, d), F32), col < row)
    carry, acc = lax.fori_loop(
        0, qi, lambda t, c: block(qi - 1 - t, c[0], c[1], None), (carry, acc))
    o_ref[...] = acc.astype(o_ref.dtype)


def _sb_prompt(proj, u2, *, batch, seq, heads, dh, q_col, k_col, v_col):
    nq = seq // Q_TILE
    kern = functools.partial(_sb_prompt_kernel, scale=dh ** -0.5)
    return pl.pallas_call(
        kern,
        grid=(batch, heads, nq),
        in_specs=[
            pl.BlockSpec((Q_TILE, dh), lambda b, h, i: (b * nq + i, q_col // dh + h)),
            pl.BlockSpec((seq, dh), lambda b, h, i: (b, k_col // dh + h)),
            pl.BlockSpec((seq, dh), lambda b, h, i: (b, v_col // dh + h)),
            pl.BlockSpec(u2.shape, lambda b, h, i: (0, 0)),
        ],
        out_specs=pl.BlockSpec((Q_TILE, dh), lambda b, h, i: (b * nq + i, h)),
        out_shape=jax.ShapeDtypeStruct((batch * seq, heads * dh), BF16),
        compiler_params=_cparams(3),
        name="sb_prompt",
    )(proj, proj, proj, u2)


def _alibi_slope(h, heads):
    s = jnp.float32(2.0 ** (-8.0 * heads / heads))
    for i in range(heads - 2, -1, -1):
        s = jnp.where(h == i, jnp.float32(2.0 ** (-8.0 * (i + 1) / heads)), s)
    return s


def _df_lambda(lam_ref, lam_init):
    l = lam_ref[...]
    s1 = jnp.sum(l[0:1] * l[1:2], axis=1, keepdims=True)
    s2 = jnp.sum(l[2:3] * l[3:4], axis=1, keepdims=True)
    return jnp.exp(s1) - jnp.exp(s2) + lam_init


def _df_finish(acc0, l0, acc1, l1, lam, g, lam_init):
    o = acc0 / l0 - lam * (acc1 / l1)
    o = o * lax.rsqrt(jnp.mean(o * o, axis=-1, keepdims=True) + RMS_EPS)
    return o * g * (1.0 - lam_init)


def _df_prompt_kernel(q_ref, k_ref, v_ref, lam_ref, g_ref, o_ref, *, scale, lam_init, heads):
    tq = q_ref.shape[0]
    dh = q_ref.shape[1] // 2
    h = pl.program_id(1)
    qi = pl.program_id(2)
    slope = _alibi_slope(h, heads)
    q = q_ref[...].astype(BF16)
    qs = (q[:, :dh], q[:, dh:])
    row = lax.broadcasted_iota(jnp.int32, (tq, tq), 0)
    col = lax.broadcasted_iota(jnp.int32, (tq, tq), 1)
    rel = (row - col).astype(F32)

    def scores(j, c, kb):
        dist = rel + ((qi - j) * tq).astype(F32)
        return _dot_nt(qs[c], kb[:, c * dh:(c + 1) * dh]) * scale - slope * dist

    off = pl.multiple_of(qi * tq, tq)
    kb = k_ref[pl.ds(off, tq), :].astype(BF16)
    vb = v_ref[pl.ds(off, tq), :].astype(BF16)
    state = []
    for c in range(2):
        s = jnp.where(rel >= 0, scores(qi, c, kb), -jnp.inf)
        m = jnp.max(s, axis=1, keepdims=True)
        p = jnp.exp(s - m)
        state += [m, jnp.sum(p, axis=1, keepdims=True), _dot(p.astype(BF16), vb)]

    def body(t, st):
        j = qi - 1 - t
        o = pl.multiple_of(j * tq, tq)
        kb = k_ref[pl.ds(o, tq), :].astype(BF16)
        vb = v_ref[pl.ds(o, tq), :].astype(BF16)
        out = []
        for c in range(2):
            m, l, acc = st[3 * c:3 * c + 3]
            s = scores(j, c, kb)
            m_new = jnp.maximum(m, jnp.max(s, axis=1, keepdims=True))
            alpha = jnp.exp(m - m_new)
            p = jnp.exp(s - m_new)
            out += [m_new, alpha * l + jnp.sum(p, axis=1, keepdims=True),
                    alpha * acc + _dot(p.astype(BF16), vb)]
        return tuple(out)

    st = lax.fori_loop(0, qi, body, tuple(state))
    lam = _df_lambda(lam_ref, lam_init)
    o_ref[...] = _df_finish(st[2], st[1], st[5], st[4], lam, g_ref[...], lam_init).astype(o_ref.dtype)


def _df_prompt(proj, lam_p, g, *, batch, seq, heads, dh, q_col, k_col, v_col, lam_init):
    nq = seq // Q_TILE
    w = 2 * dh
    kern = functools.partial(_df_prompt_kernel, scale=dh ** -0.5, lam_init=lam_init, heads=heads)
    return pl.pallas_call(
        kern,
        grid=(batch, heads, nq),
        in_specs=[
            pl.BlockSpec((Q_TILE, w), lambda b, h, i: (b * nq + i, q_col // w + h)),
            pl.BlockSpec((seq, w), lambda b, h, i: (b, k_col // w + h)),
            pl.BlockSpec((seq, w), lambda b, h, i: (b, v_col // w + h)),
            pl.BlockSpec(lam_p.shape, lambda b, h, i: (0, 0)),
            pl.BlockSpec(g.shape, lambda b, h, i: (0, 0)),
        ],
        out_specs=pl.BlockSpec((Q_TILE, w), lambda b, h, i: (b * nq + i, h)),
        out_shape=jax.ShapeDtypeStruct((batch * seq, heads * w), BF16),
        compiler_params=_cparams(3),
        name="df_prompt",
    )(proj, proj, proj, lam_p, g)


def _decode_kernel(pt_ref, qkv_ref, sbk_hbm, sbv_hbm, dfk_hbm, dfv_hbm, u2_ref, lam_ref, g_ref,
                   osb_ref, odf_ref,
                   sbk_buf, sbv_buf, dfk_buf, dfv_buf, sems, sb_carry, sb_acc, df_m, df_l, df_acc,
                   *, layer, n_pages, page, sb_heads, sb_dh, df_heads, df_dh, cols, lam_init):
    b = pl.program_id(0)
    n_chunks = n_pages // CHUNK_PAGES
    total_chunks = pl.num_programs(0) * n_chunks
    past_len = n_pages * page
    sb_q_col, sb_k_col, sb_v_col, df_q_col, df_k_col, df_v_col = cols
    sb_scale = sb_dh ** -0.5
    df_scale = df_dh ** -0.5
    dv = 2 * df_dh

    def chunk_copies(c, slot):
        seq = c // n_chunks
        first_page = n_pages - CHUNK_PAGES * (c % n_chunks + 1)
        copies = []
        for j in range(CHUNK_PAGES):
            phys = pt_ref[seq * n_pages + first_page + j]
            copies.append(pltpu.make_async_copy(sbk_hbm.at[layer, phys], sbk_buf.at[slot, j], sems.at[slot, 0]))
            copies.append(pltpu.make_async_copy(sbv_hbm.at[layer, phys], sbv_buf.at[slot, j], sems.at[slot, 1]))
            copies.append(pltpu.make_async_copy(dfk_hbm.at[layer, phys], dfk_buf.at[slot, j], sems.at[slot, 2]))
            for h in range(df_heads):
                copies.append(pltpu.make_async_copy(dfv_hbm.at[layer, phys, :, h, :],
                                                    dfv_buf.at[slot, j, h], sems.at[slot, 3]))
        return copies

    def qslice(col, width):
        return qkv_ref[:, col:col + width].astype(BF16)

    def pad_rows(x):
        return jnp.concatenate([x, jnp.zeros((page - QPAD, x.shape[1]), x.dtype)], axis=0)

    def sb_update(ks, vs, kpos0, first):
        z = jnp.concatenate(
            [_dot_nt(qslice(sb_q_col + h * sb_dh, sb_dh), ks[h]) for h in range(sb_heads)],
            axis=0) * sb_scale
        valid = None
        if first:
            qpos = past_len + (lax.broadcasted_iota(jnp.int32, z.shape, 0) & (QPAD - 1))
            valid = kpos0 + lax.broadcasted_iota(jnp.int32, z.shape, 1) < qpos
        carry = jnp.zeros((z.shape[0], SCAN_W), F32) if first else sb_carry[...]
        a, carry = _sb_weights(z, u2_ref[...], carry, valid)
        a = a.astype(BF16)
        pv = jnp.concatenate(
            [_dot(a[h * QPAD:(h + 1) * QPAD], vs[h]) for h in range(sb_heads)], axis=0)
        sb_carry[...] = carry
        sb_acc[...] = pv if first else sb_acc[...] + pv

    def df_update(ks, vs, kpos0, first):
        s = jnp.concatenate(
            [_dot_nt(qslice(df_q_col + (2 * h + c) * df_dh, df_dh), ks[h][c])
             for h in range(df_heads) for c in range(2)], axis=0) * df_scale
        ridx = lax.broadcasted_iota(jnp.int32, s.shape, 0)
        qpos = past_len + (ridx & (QPAD - 1))
        kpos = kpos0 + lax.broadcasted_iota(jnp.int32, s.shape, 1)
        dist = (qpos - kpos).astype(F32)
        slope = _alibi_slope(ridx // (2 * QPAD), df_heads)
        s = s - slope * dist
        if first:
            s = jnp.where(dist >= 0, s, -jnp.inf)
        m_blk = jnp.max(s, axis=1, keepdims=True)
        m_new = m_blk if first else jnp.maximum(df_m[...], m_blk)
        p = jnp.exp(s - m_new)
        psum = jnp.sum(p, axis=1, keepdims=True)
        p = p.astype(BF16)
        pv = jnp.concatenate(
            [_dot(p[h * 2 * QPAD:(h + 1) * 2 * QPAD], vs[h]) for h in range(df_heads)], axis=0)
        if first:
            df_l[...] = psum
            df_acc[...] = pv
        else:
            alpha = jnp.exp(df_m[...] - m_new)
            df_l[...] = alpha * df_l[...] + psum
            df_acc[...] = alpha * df_acc[...] + pv
        df_m[...] = m_new

    @pl.when(b == 0)
    def _prime():
        for cp in chunk_copies(0, 0):
            cp.start()

    ks = [pad_rows(qslice(sb_k_col + h * sb_dh, sb_dh)) for h in range(sb_heads)]
    vs = [pad_rows(qslice(sb_v_col + h * sb_dh, sb_dh)) for h in range(sb_heads)]
    sb_update(ks, vs, past_len, True)
    kd = [[pad_rows(qslice(df_k_col + (2 * h + c) * df_dh, df_dh)) for c in range(2)]
          for h in range(df_heads)]
    vd = [pad_rows(qslice(df_v_col + h * dv, dv)) for h in range(df_heads)]
    df_update(kd, vd, past_len, True)

    def cached_chunk(i, _):
        c = b * n_chunks + i
        slot = c % N_SLOTS

        @pl.when(c + 1 < total_chunks)
        def _prefetch():
            for cp in chunk_copies(c + 1, (c + 1) % N_SLOTS):
                cp.start()

        for cp in chunk_copies(c, slot):
            cp.wait()

        def head_rows(buf, off, stride):
            return jnp.concatenate([buf[slot, j, pl.ds(off, page, stride=stride), :]
                                    for j in range(CHUNK_PAGES)], axis=0).astype(BF16)

        kpos0 = (n_pages - CHUNK_PAGES * (i + 1)) * page
        ks = [head_rows(sbk_buf, h, sb_heads) for h in range(sb_heads)]
        vs = [head_rows(sbv_buf, h, sb_heads) for h in range(sb_heads)]
        sb_update(ks, vs, kpos0, False)
        kd = [[head_rows(dfk_buf, 2 * h + c, 2 * df_heads) for c in range(2)] for h in range(df_heads)]
        vd = [jnp.concatenate([dfv_buf[slot, j, h] for j in range(CHUNK_PAGES)], axis=0).astype(BF16)
              for h in range(df_heads)]
        df_update(kd, vd, kpos0, False)
        return 0

    lax.fori_loop(0, n_chunks, cached_chunk, 0)

    for h in range(sb_heads):
        osb_ref[:, h * sb_dh:(h + 1) * sb_dh] = sb_acc[h * QPAD:(h + 1) * QPAD, :].astype(osb_ref.dtype)
    lam = _df_lambda(lam_ref, lam_init)
    g = g_ref[...]
    for h in range(df_heads):
        r0 = h * 2 * QPAD
        o = _df_finish(df_acc[r0:r0 + QPAD, :], df_l[r0:r0 + QPAD, :],
                       df_acc[r0 + QPAD:r0 + 2 * QPAD, :], df_l[r0 + QPAD:r0 + 2 * QPAD, :],
                       lam, g, lam_init)
        odf_ref[:, h * dv:(h + 1) * dv] = o.astype(odf_ref.dtype)


def _decode_attn(qkv_new, page_table, cache_sb_k, cache_sb_v, cache_df_k, cache_df_v, u2, lam_p, g,
                 *, layer, cols, lam_init):
    nseq, _, width = qkv_new.shape
    n_pages = page_table.shape[1]
    _, _, page, sb_heads, sb_dh = cache_sb_k.shape
    df_heads, df_dh = cache_df_k.shape[3], cache_df_k.shape[5]
    dv = 2 * df_dh
    depth, n_phys = cache_sb_k.shape[:2]
    assert n_pages % CHUNK_PAGES == 0 and page == SCAN_W

    kern = functools.partial(_decode_kernel, layer=layer, n_pages=n_pages, page=page,
                             sb_heads=sb_heads, sb_dh=sb_dh, df_heads=df_heads, df_dh=df_dh,
                             cols=cols, lam_init=lam_init)
    hbm = pl.BlockSpec(memory_space=pl.ANY)
    grid_spec = pltpu.PrefetchScalarGridSpec(
        num_scalar_prefetch=1,
        grid=(nseq,),
        in_specs=[
            pl.BlockSpec((None, QPAD, width), lambda b, pt: (b, 0, 0)),
            hbm, hbm, hbm, hbm,
            pl.BlockSpec(u2.shape, lambda b, pt: (0, 0)),
            pl.BlockSpec(lam_p.shape, lambda b, pt: (0, 0)),
            pl.BlockSpec(g.shape, lambda b, pt: (0, 0)),
        ],
        out_specs=[
            pl.BlockSpec((None, QPAD, sb_heads * sb_dh), lambda b, pt: (b, 0, 0)),
            pl.BlockSpec((None, QPAD, df_heads * dv), lambda b, pt: (b, 0, 0)),
        ],
        scratch_shapes=[
            pltpu.VMEM((N_SLOTS, CHUNK_PAGES, page * sb_heads, sb_dh), F32),
            pltpu.VMEM((N_SLOTS, CHUNK_PAGES, page * sb_heads, sb_dh), F32),
            pltpu.VMEM((N_SLOTS, CHUNK_PAGES, page * df_heads * 2, df_dh), F32),
            pltpu.VMEM((N_SLOTS, CHUNK_PAGES, df_heads, page, dv), F32),
            pltpu.SemaphoreType.DMA((N_SLOTS, 4)),
            pltpu.VMEM((sb_heads * QPAD, SCAN_W), F32),
            pltpu.VMEM((sb_heads * QPAD, sb_dh), F32),
            pltpu.VMEM((df_heads * 2 * QPAD, 1), F32),
            pltpu.VMEM((df_heads * 2 * QPAD, 1), F32),
            pltpu.VMEM((df_heads * 2 * QPAD, dv), F32),
        ],
    )
    return pl.pallas_call(
        kern,
        grid_spec=grid_spec,
        out_shape=[jax.ShapeDtypeStruct((nseq, QPAD, sb_heads * sb_dh), BF16),
                   jax.ShapeDtypeStruct((nseq, QPAD, df_heads * dv), BF16)],
        compiler_params=_cparams(1),
        name="decode_attn",
    )(page_table.reshape(-1), qkv_new,
      cache_sb_k.reshape(depth, n_phys, page * sb_heads, sb_dh),
      cache_sb_v.reshape(depth, n_phys, page * sb_heads, sb_dh),
      cache_df_k.reshape(depth, n_phys, page * df_heads * 2, df_dh),
      cache_df_v, u2, lam_p, g)


def _conv_prompt_kernel(b_ref, c_ref, h_ref, w_ref, o_ref, hist_ref):
    u = c_ref[...] * h_ref[...]
    s = u.shape[0]
    row = lax.broadcasted_iota(jnp.int32, u.shape, 0)
    u1 = jnp.where(row >= 1, pltpu.roll(u, 1, axis=0), 0.0)
    u2 = jnp.where(row >= 2, pltpu.roll(u, 2, axis=0), 0.0)
    w = w_ref[...]
    conv = w[0:1] * u2 + w[1:2] * u1 + w[2:3] * u
    o_ref[...] = (b_ref[...] * conv).astype(o_ref.dtype)
    hist_ref[...] = u[s - 2:s, :]


def _conv_prompt(proj, conv_w, *, batch, seq, width, b_col):
    nc = width // LANES
    cb = b_col // LANES
    return pl.pallas_call(
        _conv_prompt_kernel,
        grid=(batch, nc),
        in_specs=[
            pl.BlockSpec((seq, LANES), lambda b, c: (b, cb + c)),
            pl.BlockSpec((seq, LANES), lambda b, c: (b, cb + nc + c)),
            pl.BlockSpec((seq, LANES), lambda b, c: (b, cb + 2 * nc + c)),
            pl.BlockSpec((conv_w.shape[0], LANES), lambda b, c: (0, c)),
        ],
        out_specs=[pl.BlockSpec((seq, LANES), lambda b, c: (b, c)),
                   pl.BlockSpec((None, 2, LANES), lambda b, c: (b, 0, c))],
        out_shape=[jax.ShapeDtypeStruct((batch * seq, width), BF16),
                   jax.ShapeDtypeStruct((batch, 2, width), F32)],
        compiler_params=_cparams(2),
        name="conv_prompt",
    )(proj, proj, proj, conv_w)


def _conv_sample_kernel(b_ref, c_ref, h_ref, hist_ref, w_ref, o_ref, nh_ref, *, width, steps):
    u = c_ref[...] * h_ref[...]
    ext = jnp.concatenate([hist_ref[...], u], axis=1)
    w = w_ref[...]
    b = b_ref[...]
    for q in range(steps):
        conv = (w[0:1] * ext[:, q * width:(q + 1) * width]
                + w[1:2] * ext[:, (q + 1) * width:(q + 2) * width]
                + w[2:3] * ext[:, (q + 2) * width:(q + 3) * width])
        o_ref[:, q * width:(q + 1) * width] = (b[:, q * width:(q + 1) * width] * conv).astype(o_ref.dtype)
    nh_ref[...] = ext[:, steps * width:(steps + 2) * width]


def _conv_sample(cv_b, cv_c, cv_h, hist, conv_w, *, width, steps):
    nseq = cv_b.shape[0]
    kern = functools.partial(_conv_sample_kernel, width=width, steps=steps)
    full = lambda a: pl.BlockSpec(a.shape, lambda i: (0,) * a.ndim)
    return pl.pallas_call(
        kern,
        grid=(1,),
        in_specs=[full(cv_b), full(cv_c), full(cv_h), full(hist), full(conv_w)],
        out_specs=[pl.BlockSpec((nseq, steps * width), lambda i: (0, 0)),
                   pl.BlockSpec((nseq, 2 * width), lambda i: (0, 0))],
        out_shape=[jax.ShapeDtypeStruct((nseq, steps * width), BF16),
                   jax.ShapeDtypeStruct((nseq, 2 * width), F32)],
        compiler_params=_cparams(1),
        name="conv_sample",
    )(cv_b, cv_c, cv_h, hist, conv_w)


def _merge_kernel(osb_ref, odf_ref, ocv_ref, gsb_ref, gdf_ref, gcv_ref, wsb_ref, wdf_ref, wcv_ref, o_ref):
    m = (jax.nn.sigmoid(gsb_ref[...]) * _dot(osb_ref[...], wsb_ref[...])
         + jax.nn.sigmoid(gdf_ref[...]) * _dot(odf_ref[...], wdf_ref[...])
         + jax.nn.sigmoid(gcv_ref[...]) * _dot(ocv_ref[...], wcv_ref[...]))
    o_ref[...] = m.astype(o_ref.dtype)


def _merge(o_sb, o_df, o_cv, proj, w_sb, w_df, w_cv, *, gate_col, tm):
    t = o_sb.shape[0]
    d = w_sb.shape[1]
    gb = gate_col // d
    row = lambda w: pl.BlockSpec((tm, w), lambda i: (i, 0))
    const = lambda a: pl.BlockSpec(a.shape, lambda i: (0, 0))
    return pl.pallas_call(
        _merge_kernel,
        grid=(t // tm,),
        in_specs=[row(o_sb.shape[1]), row(o_df.shape[1]), row(o_cv.shape[1]),
                  pl.BlockSpec((tm, d), lambda i: (i, gb)),
                  pl.BlockSpec((tm, d), lambda i: (i, gb + 1)),
                  pl.BlockSpec((tm, d), lambda i: (i, gb + 2)),
                  const(w_sb), const(w_df), const(w_cv)],
        out_specs=pl.BlockSpec((tm, d), lambda i: (i, 0)),
        out_shape=jax.ShapeDtypeStruct((t, d), BF16),
        compiler_params=_cparams(1),
        name="gated_merge",
    )(o_sb, o_df, o_cv, proj, proj, proj, w_sb, w_df, w_cv)


def _oproj_ln_kernel(m_ref, wo_ref, x_ref, g_ref, b_ref, xo_ref, xb_ref, *, alpha):
    y = alpha * x_ref[...] + _dot(m_ref[...], wo_ref[...])
    out = _layernorm(y, g_ref[...], b_ref[...])
    xo_ref[...] = out
    xb_ref[...] = out.astype(xb_ref.dtype)


def _oproj_ln(merged, w_o, x, g, b, *, alpha, tm):
    t, d = x.shape
    row = pl.BlockSpec((tm, d), lambda i: (i, 0))
    const = lambda a: pl.BlockSpec(a.shape, lambda i: (0, 0))
    return pl.pallas_call(
        functools.partial(_oproj_ln_kernel, alpha=alpha),
        grid=(t // tm,),
        in_specs=[row, const(w_o), row, const(g), const(b)],
        out_specs=[row, row],
        out_shape=[jax.ShapeDtypeStruct((t, d), F32), jax.ShapeDtypeStruct((t, d), BF16)],
        compiler_params=_cparams(1),
        name="oproj_ln",
    )(merged, w_o, x, g, b)


def _mlp_kernel(xb_ref, wu_ref, wd_ref, x_ref, g_ref, b_ref, xo_ref, xbo_ref, acc_ref, *, alpha):
    f = pl.program_id(1)
    h = jnp.maximum(_dot(xb_ref[...], wu_ref[...]), 0.0)
    part = _dot((h * h).astype(BF16), wd_ref[...])

    @pl.when(f == 0)
    def _():
        acc_ref[...] = part

    @pl.when(f > 0)
    def _():
        acc_ref[...] += part

    @pl.when(f == pl.num_programs(1) - 1)
    def _():
        out = _layernorm(alpha * x_ref[...] + acc_ref[...], g_ref[...], b_ref[...])
        xo_ref[...] = out
        xbo_ref[...] = out.astype(xbo_ref.dtype)


def _mlp_ln(xb, w_up, w_down, x, g, b, *, alpha, tm, tf):
    t, d = x.shape
    ff = w_up.shape[1]
    row = pl.BlockSpec((tm, d), lambda i, f: (i, 0))
    const = lambda a: pl.BlockSpec(a.shape, lambda i, f: (0, 0))
    return pl.pallas_call(
        functools.partial(_mlp_kernel, alpha=alpha),
        grid=(t // tm, ff // tf),
        in_specs=[row,
                  pl.BlockSpec((d, tf), lambda i, f: (0, f)),
                  pl.BlockSpec((tf, d), lambda i, f: (f, 0)),
                  row, const(g), const(b)],
        out_specs=[row, row],
        out_shape=[jax.ShapeDtypeStruct((t, d), F32), jax.ShapeDtypeStruct((t, d), BF16)],
        scratch_shapes=[pltpu.VMEM((tm, d), F32)],
        compiler_params=_cparams(2),
        name="mlp_ln",
    )(xb, w_up, w_down, x, g, b)


def _tile(n, pref):
    t = min(n, pref)
    while n % t:
        t //= 2
    return t


def kernel(x_prompt, x_sample, cache_sb_k, cache_sb_v, cache_df_k, cache_df_v, state_conv, page_table,
           w_in, conv_w, df_lambda, df_norm_g, w_br_sb, w_br_df, w_br_cv, w_o, ln1_g, ln1_b,
           w_up, w_down, ln2_g, ln2_b):
    batch, seq, d = x_prompt.shape
    nseq, steps, _ = x_sample.shape
    depth = w_in.shape[0]
    sb_heads, sb_dh = cache_sb_k.shape[3], cache_sb_k.shape[4]
    df_heads, df_dh = cache_df_k.shape[3], cache_df_k.shape[5]
    cv_w = state_conv.shape[3]
    sb_w = sb_heads * sb_dh
    df_w = df_heads * 2 * df_dh
    tp = batch * seq
    ts = nseq * steps
    alpha = (2 * depth) ** 0.25

    sb_q_col, sb_k_col, sb_v_col = 0, sb_w, 2 * sb_w
    df_q_col = 3 * sb_w
    df_k_col, df_v_col = df_q_col + df_w, df_q_col + 2 * df_w
    cv_col = df_q_col + 3 * df_w
    gate_col = cv_col + 3 * cv_w
    cols = (sb_q_col, sb_k_col, sb_v_col, df_q_col, df_k_col, df_v_col)

    u2 = _scan_matrix(SCAN_W)
    x = jnp.concatenate([x_prompt.reshape(tp, d), x_sample.reshape(ts, d)], axis=0)
    xb = x.astype(BF16)
    tm = _tile(tp + ts, 512)

    new_p = [[] for _ in range(5)]
    new_s = [[] for _ in range(5)]
    for l in range(depth):
        lam_init = 0.8 - 0.6 * math.exp(-0.3 * l)
        g_df = df_norm_g[l].reshape(1, -1)
        proj = _matmul(xb, w_in[l].astype(BF16), tm=tm, tn=_tile(w_in.shape[2], 1024), name="in_proj")

        o_sb_p = _sb_prompt(proj, u2, batch=batch, seq=seq, heads=sb_heads, dh=sb_dh,
                            q_col=sb_q_col, k_col=sb_k_col, v_col=sb_v_col)
        o_df_p = _df_prompt(proj, df_lambda[l], g_df, batch=batch, seq=seq, heads=df_heads, dh=df_dh,
                            q_col=df_q_col, k_col=df_k_col, v_col=df_v_col, lam_init=lam_init)
        o_cv_p, hist_p = _conv_prompt(proj, conv_w[l], batch=batch, seq=seq, width=cv_w, b_col=cv_col)

        proj_s = proj[tp:]
        qkv_new = jnp.pad(proj_s[:, :cv_col].reshape(nseq, steps, cv_col),
                          ((0, 0), (0, QPAD - steps), (0, 0)))
        o_sb_s, o_df_s = _decode_attn(qkv_new, page_table, cache_sb_k, cache_sb_v, cache_df_k,
                                      cache_df_v, u2, df_lambda[l], g_df,
                                      layer=l, cols=cols, lam_init=lam_init)
        o_sb_s = o_sb_s[:, :steps].reshape(ts, sb_w)
        o_df_s = o_df_s[:, :steps].reshape(ts, df_w)
        cvs = [proj_s[:, cv_col + i * cv_w:cv_col + (i + 1) * cv_w].reshape(nseq, steps * cv_w)
               for i in range(3)]
        o_cv_s, hist_s = _conv_sample(cvs[0], cvs[1], cvs[2], state_conv[l].reshape(nseq, 2 * cv_w),
                                      conv_w[l], width=cv_w, steps=steps)

        o_sb = jnp.concatenate([o_sb_p, o_sb_s], axis=0)
        o_df = jnp.concatenate([o_df_p, o_df_s], axis=0)
        o_cv = jnp.concatenate([o_cv_p, o_cv_s.reshape(ts, cv_w)], axis=0)
        merged = _merge(o_sb, o_df, o_cv, proj, w_br_sb[l].astype(BF16), w_br_df[l].astype(BF16),
                        w_br_cv[l].astype(BF16), gate_col=gate_col, tm=_tile(tp + ts, 256))
        x, xb = _oproj_ln(merged, w_o[l].astype(BF16), x, ln1_g[l].reshape(1, d), ln1_b[l].reshape(1, d),
                          alpha=alpha, tm=tm)
        x, xb = _mlp_ln(xb, w_up[l].astype(BF16), w_down[l].astype(BF16), x,
                        ln2_g[l].reshape(1, d), ln2_b[l].reshape(1, d),
                        alpha=alpha, tm=tm, tf=_tile(w_up.shape[2], 512))

        kv = (proj[:, sb_k_col:sb_k_col + sb_w], proj[:, sb_v_col:sb_v_col + sb_w],
              proj[:, df_k_col:df_k_col + df_w], proj[:, df_v_col:df_v_col + df_w])
        shapes = ((sb_heads, sb_dh), (sb_heads, sb_dh), (df_heads, 2, df_dh), (df_heads, 2 * df_dh))
        for i in range(4):
            new_p[i].append(kv[i][:tp].reshape(batch, seq, *shapes[i]))
            new_s[i].append(kv[i][tp:].reshape(nseq, steps, *shapes[i]))
        new_p[4].append(hist_p)
        new_s[4].append(hist_s.reshape(nseq, 2, cv_w))

    outs_p = [jnp.stack(a, 0) for a in new_p]
    outs_s = [jnp.stack(a, 0) for a in new_s]
    return (x[:tp].reshape(batch, seq, d), x[tp:].reshape(nseq, steps, d), *outs_p, *outs_s)
```

```python
import functools
import math

import jax
import jax.numpy as jnp
from jax import lax
from jax.experimental import pallas as pl
from jax.experimental.pallas import tpu as pltpu

F32 = jnp.float32
BF16 = jnp.bfloat16

LN_EPS = 1e-5
RMS_EPS = 1e-6
LANES = 128
SUBLANES = 8
VMEM_LIMIT = 56 * 1024 * 1024
Q_TILE = 256
SCAN_W = 128
QPAD = 8
CHUNK_PAGES = 2
N_SLOTS = 3


def _cparams(n_axes):
    return pltpu.CompilerParams(dimension_semantics=("arbitrary",) * n_axes,
                                vmem_limit_bytes=VMEM_LIMIT)


def _layer_spec(a, layer):
    tail = (0,) * (a.ndim - 1)
    return pl.BlockSpec((None,) + a.shape[1:], lambda *_: (layer,) + tail)


def _dot(a, b):
    return jnp.dot(a, b, preferred_element_type=F32)


def _dot_nt(a, b):
    return lax.dot_general(a, b, (((1,), (1,)), ((), ())), preferred_element_type=F32)


def _log_sigmoid(z):
    return jnp.minimum(z, 0.0) - jnp.log1p(jnp.exp(-jnp.abs(z)))


def _layernorm(y, g, b):
    mu = jnp.mean(y, axis=-1, keepdims=True)
    yc = y - mu
    var = jnp.mean(yc * yc, axis=-1, keepdims=True)
    return yc * lax.rsqrt(var + LN_EPS) * g + b


def _scan_matrix(tk):
    r = jnp.arange(tk)[:, None]
    c = jnp.arange(tk)[None, :]
    upper = (r > c).astype(BF16)
    half = jnp.concatenate([upper, jnp.ones((tk, tk), BF16)], axis=1)
    return jnp.concatenate([half, half], axis=0)


def _mm_kernel(x_ref, w_ref, o_ref, wb_ref):
    @pl.when(pl.program_id(1) == 0)
    def _():
        wb_ref[...] = w_ref[...].astype(BF16)

    o_ref[...] = _dot(x_ref[...], wb_ref[...]).astype(o_ref.dtype)


def _matmul(x, w, *, layer, tm, tn, name):
    m, k = x.shape
    n = w.shape[2]
    return pl.pallas_call(
        _mm_kernel,
        grid=(n // tn, m // tm),
        in_specs=[pl.BlockSpec((tm, k), lambda j, i: (i, 0)),
                  pl.BlockSpec((None, k, tn), lambda j, i: (layer, 0, j))],
        out_specs=pl.BlockSpec((tm, tn), lambda j, i: (i, j)),
        out_shape=jax.ShapeDtypeStruct((m, n), F32),
        scratch_shapes=[pltpu.VMEM((k, tn), BF16)],
        compiler_params=_cparams(2),
        name=name,
    )(x, w)


def _sb_weights(z, u2, carry, valid):
    w = u2.shape[0] // 2
    nseg = z.shape[1] // w
    ls = _log_sigmoid(z)
    lk = ls - z
    if valid is not None:
        lk = jnp.where(valid, lk, 0.0)
    hi = lk.astype(BF16)
    lo = (lk - hi.astype(F32)).astype(BF16)
    after = [None] * nseg
    for s in reversed(range(nseg)):
        seg = slice(s * w, (s + 1) * w)
        sc = _dot(jnp.concatenate([hi[:, seg], lo[:, seg]], axis=1), u2)
        after[s] = sc[:, :w] + carry
        carry = carry + sc[:, w:]
    a = jnp.exp(ls + (after[0] if nseg == 1 else jnp.concatenate(after, axis=1)))
    if valid is not None:
        a = jnp.where(valid, a, 0.0)
    return a, carry


def _sb_prompt_kernel(q_ref, k_ref, v_ref, u2_ref, o_ref, *, scale):
    tq, d = q_ref.shape
    qi = pl.program_id(2)
    q = q_ref[...].astype(BF16)
    u2 = u2_ref[...]

    def block(j, carry, acc, valid):
        off = pl.multiple_of(j * tq, tq)
        kb = k_ref[pl.ds(off, tq), :].astype(BF16)
        vb = v_ref[pl.ds(off, tq), :].astype(BF16)
        a, carry = _sb_weights(_dot_nt(q, kb) * scale, u2, carry, valid)
        return carry, acc + _dot(a.astype(BF16), vb)

    row = lax.broadcasted_iota(jnp.int32, (tq, tq), 0)
    col = lax.broadcasted_iota(jnp.int32, (tq, tq), 1)
    carry, acc = block(qi, jnp.zeros((tq, SCAN_W), F32), jnp.zeros((tq, d), F32), col < row)
    carry, acc = lax.fori_loop(
        0, qi, lambda t, c: block(qi - 1 - t, c[0], c[1], None), (carry, acc))
    o_ref[...] = acc.astype(o_ref.dtype)


def _sb_prompt(proj, u2, *, batch, seq, heads, dh, q_col, k_col, v_col):
    nq = seq // Q_TILE
    kern = functools.partial(_sb_prompt_kernel, scale=dh ** -0.5)
    return pl.pallas_call(
        kern,
        grid=(batch, heads, nq),
        in_specs=[
            pl.BlockSpec((Q_TILE, dh), lambda b, h, i: (b * nq + i, q_col // dh + h)),
            pl.BlockSpec((seq, dh), lambda b, h, i: (b, k_col // dh + h)),
            pl.BlockSpec((seq, dh), lambda b, h, i: (b, v_col // dh + h)),
            pl.BlockSpec(u2.shape, lambda b, h, i: (0, 0)),
        ],
        out_specs=pl.BlockSpec((Q_TILE, dh), lambda b, h, i: (b * nq + i, h)),
        out_shape=jax.ShapeDtypeStruct((batch * seq, heads * dh), BF16),
        compiler_params=_cparams(3),
        name="sb_prompt",
    )(proj, proj, proj, u2)


def _alibi_slope(h, heads):
    s = jnp.float32(2.0 ** (-8.0 * heads / heads))
    for i in range(heads - 2, -1, -1):
        s = jnp.where(h == i, jnp.float32(2.0 ** (-8.0 * (i + 1) / heads)), s)
    return s


def _df_lambda(lam_ref, lam_init):
    l = lam_ref[...]
    s1 = jnp.sum(l[0:1] * l[1:2], axis=1, keepdims=True)
    s2 = jnp.sum(l[2:3] * l[3:4], axis=1, keepdims=True)
    return jnp.exp(s1) - jnp.exp(s2) + lam_init


def _df_finish(acc0, l0, acc1, l1, lam, g, lam_init):
    o = acc0 / l0 - lam * (acc1 / l1)
    o = o * lax.rsqrt(jnp.mean(o * o, axis=-1, keepdims=True) + RMS_EPS)
    return o * g * (1.0 - lam_init)


def _df_prompt_kernel(q_ref, k_ref, v_ref, lam_ref, g_ref, o_ref, *, scale, lam_init, heads):
    tq = q_ref.shape[0]
    dh = q_ref.shape[1] // 2
    h = pl.program_id(1)
    qi = pl.program_id(2)
    slope = _alibi_slope(h, heads)
    q = q_ref[...].astype(BF16)
    qs = (q[:, :dh], q[:, dh:])
    row = lax.broadcasted_iota(jnp.int32, (tq, tq), 0)
    col = lax.broadcasted_iota(jnp.int32, (tq, tq), 1)
    rel = (row - col).astype(F32)

    def scores(j, c, kb):
        dist = rel + ((qi - j) * tq).astype(F32)
        return _dot_nt(qs[c], kb[:, c * dh:(c + 1) * dh]) * scale - slope * dist

    off = pl.multiple_of(qi * tq, tq)
    kb = k_ref[pl.ds(off, tq), :].astype(BF16)
    vb = v_ref[pl.ds(off, tq), :].astype(BF16)
    state = []
    for c in range(2):
        s = jnp.where(rel >= 0, scores(qi, c, kb), -jnp.inf)
        m = jnp.max(s, axis=1, keepdims=True)
        p = jnp.exp(s - m)
        state += [m, jnp.sum(p, axis=1, keepdims=True), _dot(p.astype(BF16), vb)]

    def body(t, st):
        j = qi - 1 - t
        o = pl.multiple_of(j * tq, tq)
        kb = k_ref[pl.ds(o, tq), :].astype(BF16)
        vb = v_ref[pl.ds(o, tq), :].astype(BF16)
        out = []
        for c in range(2):
            m, l, acc = st[3 * c:3 * c + 3]
            s = scores(j, c, kb)
            m_new = jnp.maximum(m, jnp.max(s, axis=1, keepdims=True))
            alpha = jnp.exp(m - m_new)
            p = jnp.exp(s - m_new)
            out += [m_new, alpha * l + jnp.sum(p, axis=1, keepdims=True),
                    alpha * acc + _dot(p.astype(BF16), vb)]
        return tuple(out)

    st = lax.fori_loop(0, qi, body, tuple(state))
    lam = _df_lambda(lam_ref, lam_init)
    o_ref[...] = _df_finish(st[2], st[1], st[5], st[4], lam, g_ref[...], lam_init).astype(o_ref.dtype)


def _df_prompt(proj, lam_p, g, *, layer, batch, seq, heads, dh, q_col, k_col, v_col, lam_init):
    nq = seq // Q_TILE
    w = 2 * dh
    kern = functools.partial(_df_prompt_kernel, scale=dh ** -0.5, lam_init=lam_init, heads=heads)
    return pl.pallas_call(
        kern,
        grid=(batch, heads, nq),
        in_specs=[
            pl.BlockSpec((Q_TILE, w), lambda b, h, i: (b * nq + i, q_col // w + h)),
            pl.BlockSpec((seq, w), lambda b, h, i: (b, k_col // w + h)),
            pl.BlockSpec((seq, w), lambda b, h, i: (b, v_col // w + h)),
            _layer_spec(lam_p, layer),
            _layer_spec(g, layer),
        ],
        out_specs=pl.BlockSpec((Q_TILE, w), lambda b, h, i: (b * nq + i, h)),
        out_shape=jax.ShapeDtypeStruct((batch * seq, heads * w), BF16),
        compiler_params=_cparams(3),
        name="df_prompt",
    )(proj, proj, proj, lam_p, g)


def _decode_kernel(pt_ref, qkv_ref, sbk_hbm, sbv_hbm, dfk_hbm, dfv_hbm, u2_ref, lam_ref, g_ref,
                   osb_ref, odf_ref,
                   sbk_buf, sbv_buf, dfk_buf, dfv_buf, sems, sb_carry, sb_acc, df_m, df_l, df_acc,
                   *, layer, n_seq, n_pages, page, sb_heads, sb_dh, df_heads, df_dh, cols, lam_init):
    b = pl.program_id(0)
    n_chunks = n_pages // CHUNK_PAGES
    total_chunks = n_seq * n_chunks
    past_len = n_pages * page
    sb_q_col, sb_k_col, sb_v_col, df_q_col, df_k_col, df_v_col = cols
    sb_scale = sb_dh ** -0.5
    df_scale = df_dh ** -0.5
    dv = 2 * df_dh

    def chunk_copies(c, slot):
        seq = c // n_chunks
        first_page = n_pages - CHUNK_PAGES * (c % n_chunks + 1)
        copies = []
        for j in range(CHUNK_PAGES):
            phys = pt_ref[seq * n_pages + first_page + j]
            copies.append(pltpu.make_async_copy(sbk_hbm.at[layer, phys], sbk_buf.at[slot, j], sems.at[slot, 0]))
            copies.append(pltpu.make_async_copy(sbv_hbm.at[layer, phys], sbv_buf.at[slot, j], sems.at[slot, 1]))
            copies.append(pltpu.make_async_copy(dfk_hbm.at[layer, phys], dfk_buf.at[slot, j], sems.at[slot, 2]))
            copies.append(pltpu.make_async_copy(dfv_hbm.at[layer, phys], dfv_buf.at[slot, j], sems.at[slot, 3]))
        return copies

    def qslice(col, width):
        return qkv_ref[:, col:col + width].astype(BF16)

    def pad_rows(x):
        return jnp.concatenate([x, jnp.zeros((page - QPAD, x.shape[1]), x.dtype)], axis=0)

    def sb_update(ks, vs, kpos0, first):
        z = jnp.concatenate(
            [_dot_nt(qslice(sb_q_col + h * sb_dh, sb_dh), ks[h]) for h in range(sb_heads)],
            axis=0) * sb_scale
        valid = None
        if first:
            qpos = past_len + (lax.broadcasted_iota(jnp.int32, z.shape, 0) & (QPAD - 1))
            valid = kpos0 + lax.broadcasted_iota(jnp.int32, z.shape, 1) < qpos
        carry = jnp.zeros((z.shape[0], SCAN_W), F32) if first else sb_carry[...]
        a, carry = _sb_weights(z, u2_ref[...], carry, valid)
        a = a.astype(BF16)
        pv = jnp.concatenate(
            [_dot(a[h * QPAD:(h + 1) * QPAD], vs[h]) for h in range(sb_heads)], axis=0)
        sb_carry[...] = carry
        sb_acc[...] = pv if first else sb_acc[...] + pv

    def df_update(ks, vs, kpos0, first):
        s = jnp.concatenate(
            [_dot_nt(qslice(df_q_col + (2 * h + c) * df_dh, df_dh), ks[h][c])
             for h in range(df_heads) for c in range(2)], axis=0) * df_scale
        ridx = lax.broadcasted_iota(jnp.int32, s.shape, 0)
        qpos = past_len + (ridx & (QPAD - 1))
        kpos = kpos0 + lax.broadcasted_iota(jnp.int32, s.shape, 1)
        dist = (qpos - kpos).astype(F32)
        slope = _alibi_slope(ridx // (2 * QPAD), df_heads)
        s = s - slope * dist
        if first:
            s = jnp.where(dist >= 0, s, -jnp.inf)
        m_blk = jnp.max(s, axis=1, keepdims=True)
        m_new = m_blk if first else jnp.maximum(df_m[...], m_blk)
        p = jnp.exp(s - m_new)
        psum = jnp.sum(p, axis=1, keepdims=True)
        p = p.astype(BF16)
        pv = jnp.concatenate(
            [_dot(p[h * 2 * QPAD:(h + 1) * 2 * QPAD], vs[h]) for h in range(df_heads)], axis=0)
        if first:
            df_l[...] = psum
            df_acc[...] = pv
        else:
            alpha = jnp.exp(df_m[...] - m_new)
            df_l[...] = alpha * df_l[...] + psum
            df_acc[...] = alpha * df_acc[...] + pv
        df_m[...] = m_new

    ahead = N_SLOTS - 1

    @pl.when(b == 0)
    def _prime():
        for c in range(min(ahead, total_chunks)):
            for cp in chunk_copies(c, c % N_SLOTS):
                cp.start()

    ks = [pad_rows(qslice(sb_k_col + h * sb_dh, sb_dh)) for h in range(sb_heads)]
    vs = [pad_rows(qslice(sb_v_col + h * sb_dh, sb_dh)) for h in range(sb_heads)]
    sb_update(ks, vs, past_len, True)
    kd = [[pad_rows(qslice(df_k_col + (2 * h + c) * df_dh, df_dh)) for c in range(2)]
          for h in range(df_heads)]
    vd = [pad_rows(qslice(df_v_col + h * dv, dv)) for h in range(df_heads)]
    df_update(kd, vd, past_len, True)

    def cached_chunk(i, _):
        c = b * n_chunks + i
        slot = c % N_SLOTS

        @pl.when(c + ahead < total_chunks)
        def _prefetch():
            for cp in chunk_copies(c + ahead, (c + ahead) % N_SLOTS):
                cp.start()

        for cp in chunk_copies(c, slot):
            cp.wait()

        def head_rows(buf, off, stride):
            return jnp.concatenate([buf[slot, j, pl.ds(off, page, stride=stride), :]
                                    for j in range(CHUNK_PAGES)], axis=0).astype(BF16)

        kpos0 = (n_pages - CHUNK_PAGES * (i + 1)) * page
        ks = [head_rows(sbk_buf, h, sb_heads) for h in range(sb_heads)]
        vs = [head_rows(sbv_buf, h, sb_heads) for h in range(sb_heads)]
        sb_update(ks, vs, kpos0, False)
        kd = [[head_rows(dfk_buf, 2 * h + c, 2 * df_heads) for c in range(2)] for h in range(df_heads)]
        vd = [jnp.concatenate([head_rows(dfv_buf, e * df_heads + h, 2 * df_heads) for e in range(2)], axis=1)
              for h in range(df_heads)]
        df_update(kd, vd, kpos0, False)
        return 0

    lax.fori_loop(0, n_chunks, cached_chunk, 0)

    for h in range(sb_heads):
        osb_ref[:, h * sb_dh:(h + 1) * sb_dh] = sb_acc[h * QPAD:(h + 1) * QPAD, :].astype(osb_ref.dtype)
    lam = _df_lambda(lam_ref, lam_init)
    g = g_ref[...]
    for h in range(df_heads):
        r0 = h * 2 * QPAD
        o = _df_finish(df_acc[r0:r0 + QPAD, :], df_l[r0:r0 + QPAD, :],
                       df_acc[r0 + QPAD:r0 + 2 * QPAD, :], df_l[r0 + QPAD:r0 + 2 * QPAD, :],
                       lam, g, lam_init)
        odf_ref[:, h * dv:(h + 1) * dv] = o.astype(odf_ref.dtype)


def _decode_attn(qkv_new, page_table, cache_sb_k, cache_sb_v, cache_df_k, cache_df_v, u2, lam_p, g,
                 *, layer, cols, lam_init):
    nseq, _, width = qkv_new.shape
    n_pages = page_table.shape[1]
    _, _, page, sb_heads, sb_dh = cache_sb_k.shape
    df_heads, df_dh = cache_df_k.shape[3], cache_df_k.shape[5]
    dv = 2 * df_dh
    depth, n_phys = cache_sb_k.shape[:2]
    assert n_pages % CHUNK_PAGES == 0 and page == SCAN_W

    kern = functools.partial(_decode_kernel, layer=layer, n_seq=nseq, n_pages=n_pages, page=page,
                             sb_heads=sb_heads, sb_dh=sb_dh, df_heads=df_heads, df_dh=df_dh,
                             cols=cols, lam_init=lam_init)
    hbm = pl.BlockSpec(memory_space=pl.ANY)
    grid_spec = pltpu.PrefetchScalarGridSpec(
        num_scalar_prefetch=1,
        grid=(nseq,),
        in_specs=[
            pl.BlockSpec((None, QPAD, width), lambda b, pt: (b, 0, 0)),
            hbm, hbm, hbm, hbm,
            pl.BlockSpec(u2.shape, lambda b, pt: (0, 0)),
            _layer_spec(lam_p, layer),
            _layer_spec(g, layer),
        ],
        out_specs=[
            pl.BlockSpec((None, QPAD, sb_heads * sb_dh), lambda b, pt: (b, 0, 0)),
            pl.BlockSpec((None, QPAD, df_heads * dv), lambda b, pt: (b, 0, 0)),
        ],
        scratch_shapes=[
            pltpu.VMEM((N_SLOTS, CHUNK_PAGES, page * sb_heads, sb_dh), F32),
            pltpu.VMEM((N_SLOTS, CHUNK_PAGES, page * sb_heads, sb_dh), F32),
            pltpu.VMEM((N_SLOTS, CHUNK_PAGES, page * df_heads * 2, df_dh), F32),
            pltpu.VMEM((N_SLOTS, CHUNK_PAGES, page * 2 * df_heads, df_dh), F32),
            pltpu.SemaphoreType.DMA((N_SLOTS, 4)),
            pltpu.VMEM((sb_heads * QPAD, SCAN_W), F32),
            pltpu.VMEM((sb_heads * QPAD, sb_dh), F32),
            pltpu.VMEM((df_heads * 2 * QPAD, 1), F32),
            pltpu.VMEM((df_heads * 2 * QPAD, 1), F32),
            pltpu.VMEM((df_heads * 2 * QPAD, dv), F32),
        ],
    )
    df_v_rows = (cache_df_v.reshape(depth, n_phys, page, df_heads, 2, df_dh).swapaxes(3, 4)
                 .reshape(depth, n_phys, page * 2 * df_heads, df_dh))
    return pl.pallas_call(
        kern,
        grid_spec=grid_spec,
        out_shape=[jax.ShapeDtypeStruct((nseq, QPAD, sb_heads * sb_dh), BF16),
                   jax.ShapeDtypeStruct((nseq, QPAD, df_heads * dv), BF16)],
        compiler_params=_cparams(1),
        name="decode_attn",
    )(page_table.reshape(-1), qkv_new,
      cache_sb_k.reshape(depth, n_phys, page * sb_heads, sb_dh),
      cache_sb_v.reshape(depth, n_phys, page * sb_heads, sb_dh),
      cache_df_k.reshape(depth, n_phys, page * df_heads * 2, df_dh),
      df_v_rows, u2, lam_p, g)


def _conv_prompt_kernel(b_ref, c_ref, h_ref, w_ref, o_ref, hist_ref):
    u = c_ref[...] * h_ref[...]
    s = u.shape[0]
    row = lax.broadcasted_iota(jnp.int32, u.shape, 0)
    u1 = jnp.where(row >= 1, pltpu.roll(u, 1, axis=0), 0.0)
    u2 = jnp.where(row >= 2, pltpu.roll(u, 2, axis=0), 0.0)
    w = w_ref[...]
    conv = w[0:1] * u2 + w[1:2] * u1 + w[2:3] * u
    o_ref[...] = (b_ref[...] * conv).astype(o_ref.dtype)
    hist_ref[...] = u[s - 2:s, :]


def _conv_prompt(proj, conv_w, *, layer, batch, seq, width, b_col):
    nc = width // LANES
    cb = b_col // LANES
    return pl.pallas_call(
        _conv_prompt_kernel,
        grid=(batch, nc),
        in_specs=[
            pl.BlockSpec((seq, LANES), lambda b, c: (b, cb + c)),
            pl.BlockSpec((seq, LANES), lambda b, c: (b, cb + nc + c)),
            pl.BlockSpec((seq, LANES), lambda b, c: (b, cb + 2 * nc + c)),
            pl.BlockSpec((None, conv_w.shape[1], LANES), lambda b, c: (layer, 0, c)),
        ],
        out_specs=[pl.BlockSpec((seq, LANES), lambda b, c: (b, c)),
                   pl.BlockSpec((None, 2, LANES), lambda b, c: (b, 0, c))],
        out_shape=[jax.ShapeDtypeStruct((batch * seq, width), BF16),
                   jax.ShapeDtypeStruct((batch, 2, width), F32)],
        compiler_params=_cparams(2),
        name="conv_prompt",
    )(proj, proj, proj, conv_w)


def _conv_sample_kernel(b_ref, c_ref, h_ref, hist_ref, w_ref, o_ref, nh_ref, *, width, steps):
    u = c_ref[...] * h_ref[...]
    ext = jnp.concatenate([hist_ref[...], u], axis=1)
    w = w_ref[...]
    b = b_ref[...]
    for q in range(steps):
        conv = (w[0:1] * ext[:, q * width:(q + 1) * width]
                + w[1:2] * ext[:, (q + 1) * width:(q + 2) * width]
                + w[2:3] * ext[:, (q + 2) * width:(q + 3) * width])
        o_ref[:, q * width:(q + 1) * width] = (b[:, q * width:(q + 1) * width] * conv).astype(o_ref.dtype)
    nh_ref[...] = ext[:, steps * width:(steps + 2) * width]


def _conv_sample(cv_b, cv_c, cv_h, hist, conv_w, *, layer, width, steps):
    nseq = cv_b.shape[0]
    kern = functools.partial(_conv_sample_kernel, width=width, steps=steps)
    full = lambda a: pl.BlockSpec(a.shape, lambda i: (0,) * a.ndim)
    return pl.pallas_call(
        kern,
        grid=(1,),
        in_specs=[full(cv_b), full(cv_c), full(cv_h), _layer_spec(hist, layer), _layer_spec(conv_w, layer)],
        out_specs=[pl.BlockSpec((nseq, steps * width), lambda i: (0, 0)),
                   pl.BlockSpec((nseq, 2 * width), lambda i: (0, 0))],
        out_shape=[jax.ShapeDtypeStruct((nseq, steps * width), BF16),
                   jax.ShapeDtypeStruct((nseq, 2 * width), F32)],
        compiler_params=_cparams(1),
        name="conv_sample",
    )(cv_b, cv_c, cv_h, hist, conv_w)


def _merge_kernel(osb_ref, odf_ref, ocv_ref, gsb_ref, gdf_ref, gcv_ref, wsb_ref, wdf_ref, wcv_ref, o_ref):
    m = (jax.nn.sigmoid(gsb_ref[...]) * _dot(osb_ref[...], wsb_ref[...])
         + jax.nn.sigmoid(gdf_ref[...]) * _dot(odf_ref[...], wdf_ref[...])
         + jax.nn.sigmoid(gcv_ref[...]) * _dot(ocv_ref[...], wcv_ref[...]))
    o_ref[...] = m.astype(o_ref.dtype)


def _merge(o_sb, o_df, o_cv, proj, w_sb, w_df, w_cv, *, layer, gate_col, tm):
    t = o_sb.shape[0]
    d = w_sb.shape[2]
    gb = gate_col // d
    row = lambda w: pl.BlockSpec((tm, w), lambda i: (i, 0))
    return pl.pallas_call(
        _merge_kernel,
        grid=(t // tm,),
        in_specs=[row(o_sb.shape[1]), row(o_df.shape[1]), row(o_cv.shape[1]),
                  pl.BlockSpec((tm, d), lambda i: (i, gb)),
                  pl.BlockSpec((tm, d), lambda i: (i, gb + 1)),
                  pl.BlockSpec((tm, d), lambda i: (i, gb + 2)),
                  _layer_spec(w_sb, layer), _layer_spec(w_df, layer), _layer_spec(w_cv, layer)],
        out_specs=pl.BlockSpec((tm, d), lambda i: (i, 0)),
        out_shape=jax.ShapeDtypeStruct((t, d), BF16),
        compiler_params=_cparams(1),
        name="gated_merge",
    )(o_sb, o_df, o_cv, proj, proj, proj, w_sb, w_df, w_cv)


def _oproj_ln_kernel(m_ref, wo_ref, x_ref, g_ref, b_ref, xo_ref, xb_ref, *, alpha):
    y = alpha * x_ref[...] + _dot(m_ref[...], wo_ref[...])
    out = _layernorm(y, g_ref[...], b_ref[...])
    xo_ref[...] = out
    xb_ref[...] = out.astype(xb_ref.dtype)


def _oproj_ln(merged, w_o, x, g, b, *, layer, alpha, tm):
    t, d = x.shape
    row = pl.BlockSpec((tm, d), lambda i: (i, 0))
    return pl.pallas_call(
        functools.partial(_oproj_ln_kernel, alpha=alpha),
        grid=(t // tm,),
        in_specs=[row, _layer_spec(w_o, layer), row, _layer_spec(g, layer), _layer_spec(b, layer)],
        out_specs=[row, row],
        out_shape=[jax.ShapeDtypeStruct((t, d), F32), jax.ShapeDtypeStruct((t, d), BF16)],
        compiler_params=_cparams(1),
        name="oproj_ln",
    )(merged, w_o, x, g, b)


def _mlp_kernel(xb_ref, wu_ref, wd_ref, x_ref, g_ref, b_ref, xo_ref, xbo_ref, acc_ref, *, alpha):
    f = pl.program_id(1)
    h = jnp.maximum(_dot(xb_ref[...], wu_ref[...]), 0.0)
    part = _dot((h * h).astype(BF16), wd_ref[...])

    @pl.when(f == 0)
    def _():
        acc_ref[...] = part

    @pl.when(f > 0)
    def _():
        acc_ref[...] += part

    @pl.when(f == pl.num_programs(1) - 1)
    def _():
        out = _layernorm(alpha * x_ref[...] + acc_ref[...], g_ref[...], b_ref[...])
        xo_ref[...] = out
        xbo_ref[...] = out.astype(xbo_ref.dtype)


def _mlp_ln(xb, w_up, w_down, x, g, b, *, layer, alpha, tm, tf):
    t, d = x.shape
    ff = w_up.shape[2]
    row = pl.BlockSpec((tm, d), lambda i, f: (i, 0))
    return pl.pallas_call(
        functools.partial(_mlp_kernel, alpha=alpha),
        grid=(t // tm, ff // tf),
        in_specs=[row,
                  pl.BlockSpec((None, d, tf), lambda i, f: (layer, 0, f)),
                  pl.BlockSpec((None, tf, d), lambda i, f: (layer, f, 0)),
                  row, _layer_spec(g, layer), _layer_spec(b, layer)],
        out_specs=[row, row],
        out_shape=[jax.ShapeDtypeStruct((t, d), F32), jax.ShapeDtypeStruct((t, d), BF16)],
        scratch_shapes=[pltpu.VMEM((tm, d), F32)],
        compiler_params=_cparams(2),
        name="mlp_ln",
    )(xb, w_up, w_down, x, g, b)


def _kv_out_kernel(*refs, depth, sb_heads, sb_dh, df_heads, df_dh, n_df_blocks):
    per_layer = 2 + 2 * n_df_blocks
    o_sbk, o_sbv, o_dfk, o_dfv = refs[depth * per_layer:]
    tm, blk = refs[0].shape

    def emit(layer_refs):
        sbk_ref, sbv_ref = layer_refs[0], layer_refs[1]
        dfk_refs = layer_refs[2:2 + n_df_blocks]
        dfv_refs = layer_refs[2 + n_df_blocks:]

        def col(block_refs, c0, width):
            return block_refs[c0 // blk][:, c0 % blk:c0 % blk + width]

        for h in range(sb_heads):
            o_sbk[pl.ds(h, tm, stride=sb_heads), :] = sbk_ref[:, h * sb_dh:(h + 1) * sb_dh]
            o_sbv[pl.ds(h, tm, stride=sb_heads), :] = sbv_ref[:, h * sb_dh:(h + 1) * sb_dh]
        slots = 2 * df_heads
        for h in range(df_heads):
            for c in range(2):
                src = (2 * h + c) * df_dh
                o_dfk[pl.ds(2 * h + c, tm, stride=slots), :] = col(dfk_refs, src, df_dh)
                o_dfv[pl.ds(c * df_heads + h, tm, stride=slots), :] = col(dfv_refs, src, df_dh)

    for l in range(depth):
        pl.when(pl.program_id(0) == l)(
            functools.partial(emit, refs[l * per_layer:(l + 1) * per_layer]))


def _kv_out(projs, *, sb_heads, sb_dh, df_heads, df_dh, k_cols, tm):
    depth = len(projs)
    t = projs[0].shape[0]
    nt = t // tm
    sb_w = sb_heads * sb_dh
    df_w = df_heads * 2 * df_dh
    sb_k_col, sb_v_col, df_k_col, df_v_col = k_cols
    n_df_blocks = df_w // sb_w
    col0s = ([sb_k_col, sb_v_col] + [df_k_col + n * sb_w for n in range(n_df_blocks)]
             + [df_v_col + n * sb_w for n in range(n_df_blocks)])

    def blk(layer, c0):
        def index_map(l, i):
            return (jnp.where(l == layer, i, jnp.where(l < layer, 0, nt - 1)), c0 // sb_w)
        return pl.BlockSpec((tm, sb_w), index_map)

    in_specs = [blk(l, c0) for l in range(depth) for c0 in col0s]
    args = [p for p in projs for _ in col0s]
    rows = (sb_heads, sb_heads, 2 * df_heads, 2 * df_heads)
    widths = (sb_dh, sb_dh, df_dh, df_dh)
    kern = functools.partial(_kv_out_kernel, depth=depth, sb_heads=sb_heads, sb_dh=sb_dh,
                             df_heads=df_heads, df_dh=df_dh, n_df_blocks=n_df_blocks)
    return pl.pallas_call(
        kern,
        grid=(depth, nt),
        in_specs=in_specs,
        out_specs=[pl.BlockSpec((None, tm * r, w), lambda l, i: (l, i, 0)) for r, w in zip(rows, widths)],
        out_shape=[jax.ShapeDtypeStruct((depth, t * r, w), F32) for r, w in zip(rows, widths)],
        compiler_params=_cparams(2),
        name="kv_out",
    )(*args)


def _tile(n, pref):
    t = min(n, pref)
    while n % t:
        t //= 2
    return t


def kernel(x_prompt, x_sample, cache_sb_k, cache_sb_v, cache_df_k, cache_df_v, state_conv, page_table,
           w_in, conv_w, df_lambda, df_norm_g, w_br_sb, w_br_df, w_br_cv, w_o, ln1_g, ln1_b,
           w_up, w_down, ln2_g, ln2_b):
    batch, seq, d = x_prompt.shape
    nseq, steps, _ = x_sample.shape
    depth = w_in.shape[0]
    sb_heads, sb_dh = cache_sb_k.shape[3], cache_sb_k.shape[4]
    df_heads, df_dh = cache_df_k.shape[3], cache_df_k.shape[5]
    cv_w = state_conv.shape[3]
    sb_w = sb_heads * sb_dh
    df_w = df_heads * 2 * df_dh
    tp = batch * seq
    ts = nseq * steps
    alpha = (2 * depth) ** 0.25

    sb_q_col, sb_k_col, sb_v_col = 0, sb_w, 2 * sb_w
    df_q_col = 3 * sb_w
    df_k_col, df_v_col = df_q_col + df_w, df_q_col + 2 * df_w
    cv_col = df_q_col + 3 * df_w
    gate_col = cv_col + 3 * cv_w
    cols = (sb_q_col, sb_k_col, sb_v_col, df_q_col, df_k_col, df_v_col)

    u2 = _scan_matrix(SCAN_W)
    xs = [x_prompt.reshape(tp, d), x_sample.reshape(ts, d)]
    xbs = [x.astype(BF16) for x in xs]
    tms = [_tile(tp, 512), _tile(ts, 512)]
    kv_kw = dict(sb_heads=sb_heads, sb_dh=sb_dh, df_heads=df_heads, df_dh=df_dh,
                 k_cols=(sb_k_col, sb_v_col, df_k_col, df_v_col))

    g_df = df_norm_g.reshape(depth, 1, -1)
    w_br = [w.astype(BF16) for w in (w_br_sb, w_br_df, w_br_cv)]
    w_o_b, w_up_b, w_down_b = w_o.astype(BF16), w_up.astype(BF16), w_down.astype(BF16)
    ln = [a.reshape(depth, 1, d) for a in (ln1_g, ln1_b, ln2_g, ln2_b)]
    hist_in = state_conv.reshape(depth, nseq, 2 * cv_w)

    all_projs = [[], []]
    hists = [[], []]
    for l in range(depth):
        lam_init = 0.8 - 0.6 * math.exp(-0.3 * l)
        projs = [_matmul(xbs[s], w_in, layer=l, tm=tms[s], tn=_tile(w_in.shape[2], 1024), name="in_proj")
                 for s in range(2)]
        for s in range(2):
            all_projs[s].append(projs[s])

        o_sb_p = _sb_prompt(projs[0], u2, batch=batch, seq=seq, heads=sb_heads, dh=sb_dh,
                            q_col=sb_q_col, k_col=sb_k_col, v_col=sb_v_col)
        o_df_p = _df_prompt(projs[0], df_lambda, g_df, layer=l, batch=batch, seq=seq, heads=df_heads,
                            dh=df_dh, q_col=df_q_col, k_col=df_k_col, v_col=df_v_col, lam_init=lam_init)
        o_cv_p, hist_p = _conv_prompt(projs[0], conv_w, layer=l, batch=batch, seq=seq, width=cv_w,
                                      b_col=cv_col)

        proj_s = projs[1]
        qkv_new = jnp.pad(proj_s[:, :cv_col].reshape(nseq, steps, cv_col),
                          ((0, 0), (0, QPAD - steps), (0, 0)))
        o_sb_s, o_df_s = _decode_attn(qkv_new, page_table, cache_sb_k, cache_sb_v, cache_df_k,
                                      cache_df_v, u2, df_lambda, g_df,
                                      layer=l, cols=cols, lam_init=lam_init)
        o_sb_s = o_sb_s[:, :steps].reshape(ts, sb_w)
        o_df_s = o_df_s[:, :steps].reshape(ts, df_w)
        cvs = [proj_s[:, cv_col + i * cv_w:cv_col + (i + 1) * cv_w].reshape(nseq, steps * cv_w)
               for i in range(3)]
        o_cv_s, hist_s = _conv_sample(cvs[0], cvs[1], cvs[2], hist_in, conv_w, layer=l, width=cv_w,
                                      steps=steps)
        hists[0].append(hist_p)
        hists[1].append(hist_s.reshape(nseq, 2, cv_w))

        branches = [(o_sb_p, o_df_p, o_cv_p), (o_sb_s, o_df_s, o_cv_s.reshape(ts, cv_w))]
        for s in range(2):
            merged = _merge(*branches[s], projs[s], *w_br, layer=l, gate_col=gate_col,
                            tm=_tile(xs[s].shape[0], 256))
            xs[s], xbs[s] = _oproj_ln(merged, w_o_b, xs[s], ln[0], ln[1], layer=l, alpha=alpha, tm=tms[s])
            xs[s], xbs[s] = _mlp_ln(xbs[s], w_up_b, w_down_b, xs[s], ln[2], ln[3], layer=l,
                                    alpha=alpha, tm=tms[s], tf=_tile(w_up.shape[2], 1024))

    def cache_entries(flat, lead):
        sbk, sbv, dfk, dfv = flat
        dfv = dfv.reshape(depth, *lead, 2, df_heads, df_dh).swapaxes(-3, -2)
        return (sbk.reshape(depth, *lead, sb_heads, sb_dh), sbv.reshape(depth, *lead, sb_heads, sb_dh),
                dfk.reshape(depth, *lead, df_heads, 2, df_dh), dfv.reshape(depth, *lead, df_heads, 2 * df_dh))

    kv_flat = [_kv_out(all_projs[s], tm=tms[s], **kv_kw) for s in range(2)]
    return (xs[0].reshape(batch, seq, d), xs[1].reshape(nseq, steps, d),
            *cache_entries(kv_flat[0], (batch, seq)), jnp.stack(hists[0], 0),
            *cache_entries(kv_flat[1], (nseq, steps)), jnp.stack(hists[1], 0))
```

```python
import functools
import math

import jax
import jax.numpy as jnp
from jax import lax
from jax.experimental import pallas as pl
from jax.experimental.pallas import tpu as pltpu

F32 = jnp.float32
BF16 = jnp.bfloat16

LN_EPS = 1e-5
RMS_EPS = 1e-6
LANES = 128
SUBLANES = 8
VMEM_LIMIT = 56 * 1024 * 1024
Q_TILE = 256
SCAN_W = 128
QPAD = 8
SB_HEADS_PER_STEP = 4
DF_HEADS_PER_STEP = 2
CHUNK_PAGES = 4
N_SLOTS = 3


def _cparams(n_axes):
    return pltpu.CompilerParams(dimension_semantics=("arbitrary",) * n_axes,
                                vmem_limit_bytes=VMEM_LIMIT)


def _layer_spec(a, layer):
    tail = (0,) * (a.ndim - 1)
    return pl.BlockSpec((None,) + a.shape[1:], lambda *_: (layer,) + tail)


def _dot(a, b):
    return jnp.dot(a, b, preferred_element_type=F32)


def _dot_nt(a, b):
    return lax.dot_general(a, b, (((1,), (1,)), ((), ())), preferred_element_type=F32)


def _log_sigmoid(z):
    return jnp.minimum(z, 0.0) - jnp.log(1.0 + jnp.exp(-jnp.abs(z)))


def _layernorm(y, g, b):
    mu = jnp.mean(y, axis=-1, keepdims=True)
    yc = y - mu
    var = jnp.mean(yc * yc, axis=-1, keepdims=True)
    return yc * lax.rsqrt(var + LN_EPS) * g + b


def _scan_matrix(tk):
    r = jnp.arange(tk)[:, None]
    c = jnp.arange(tk)[None, :]
    upper = (r > c).astype(BF16)
    half = jnp.concatenate([upper, jnp.ones((tk, tk), BF16)], axis=1)
    return jnp.concatenate([half, half], axis=0)


def _mm_kernel(x_ref, w_ref, o_ref, wb_ref):
    @pl.when(pl.program_id(1) == 0)
    def _():
        wb_ref[...] = w_ref[...].astype(BF16)

    o_ref[...] = _dot(x_ref[...], wb_ref[...]).astype(o_ref.dtype)


def _matmul(x, w, *, layer, tm, tn, name):
    m, k = x.shape
    n = w.shape[2]
    return pl.pallas_call(
        _mm_kernel,
        grid=(n // tn, m // tm),
        in_specs=[pl.BlockSpec((tm, k), lambda j, i: (i, 0)),
                  pl.BlockSpec((None, k, tn), lambda j, i: (layer, 0, j))],
        out_specs=pl.BlockSpec((tm, tn), lambda j, i: (i, j)),
        out_shape=jax.ShapeDtypeStruct((m, n), F32),
        scratch_shapes=[pltpu.VMEM((k, tn), BF16)],
        compiler_params=_cparams(2),
        name=name,
    )(x, w)


def _sb_weights(z, u2, carry, valid):
    w = u2.shape[0] // 2
    nseg = z.shape[1] // w
    ls = _log_sigmoid(z)
    lk = ls - z
    if valid is not None:
        lk = jnp.where(valid, lk, 0.0)
    hi = lk.astype(BF16)
    lo = (lk - hi.astype(F32)).astype(BF16)
    after = [None] * nseg
    for s in reversed(range(nseg)):
        seg = slice(s * w, (s + 1) * w)
        sc = _dot(jnp.concatenate([hi[:, seg], lo[:, seg]], axis=1), u2)
        after[s] = sc[:, :w] + carry
        carry = carry + sc[:, w:]
    a = jnp.exp(ls + (after[0] if nseg == 1 else jnp.concatenate(after, axis=1)))
    if valid is not None:
        a = jnp.where(valid, a, 0.0)
    return a, carry


def _sb_prompt_kernel(q_ref, k_ref, v_ref, u2_ref, o_ref, carry_ref, acc_ref, *, scale, dh):
    tq = q_ref.shape[0]
    hp = q_ref.shape[1] // dh
    qi = pl.program_id(2)
    heads = [slice(h * dh, (h + 1) * dh) for h in range(hp)]
    qs = [q_ref[:, hs].astype(BF16) for hs in heads]
    u2 = u2_ref[...]

    def block(j, diagonal):
        off = pl.multiple_of(j * tq, tq)
        valid = None
        if diagonal:
            valid = (lax.broadcasted_iota(jnp.int32, (tq, tq), 1)
                     < lax.broadcasted_iota(jnp.int32, (tq, tq), 0))
        for h, hs in enumerate(heads):
            kb = k_ref[pl.ds(off, tq), hs].astype(BF16)
            vb = v_ref[pl.ds(off, tq), hs].astype(BF16)
            carry = jnp.zeros((tq, SCAN_W), F32) if diagonal else carry_ref[h]
            a, carry = _sb_weights(_dot_nt(qs[h], kb) * scale, u2, carry, valid)
            pv = _dot(a.astype(BF16), vb)
            carry_ref[h] = carry
            acc_ref[h] = pv if diagonal else acc_ref[h] + pv

    block(qi, True)

    def body(t, _):
        block(qi - 1 - t, False)
        return 0

    lax.fori_loop(0, qi, body, 0)
    for h, hs in enumerate(heads):
        o_ref[:, hs] = acc_ref[h].astype(o_ref.dtype)


def _sb_prompt(proj, u2, *, batch, seq, heads, dh, q_col, k_col, v_col):
    nq = seq // Q_TILE
    hp = math.gcd(heads, SB_HEADS_PER_STEP)
    w = hp * dh
    kern = functools.partial(_sb_prompt_kernel, scale=dh ** -0.5, dh=dh)
    return pl.pallas_call(
        kern,
        grid=(batch, heads // hp, nq),
        in_specs=[
            pl.BlockSpec((Q_TILE, w), lambda b, h, i: (b * nq + i, q_col // w + h)),
            pl.BlockSpec((seq, w), lambda b, h, i: (b, k_col // w + h)),
            pl.BlockSpec((seq, w), lambda b, h, i: (b, v_col // w + h)),
            pl.BlockSpec(u2.shape, lambda b, h, i: (0, 0)),
        ],
        out_specs=pl.BlockSpec((Q_TILE, w), lambda b, h, i: (b * nq + i, h)),
        out_shape=jax.ShapeDtypeStruct((batch * seq, heads * dh), BF16),
        scratch_shapes=[pltpu.VMEM((hp, Q_TILE, SCAN_W), F32),
                        pltpu.VMEM((hp, Q_TILE, dh), F32)],
        compiler_params=_cparams(3),
        name="sb_prompt",
    )(proj, proj, proj, u2)


def _alibi_slope(h, heads):
    s = jnp.float32(2.0 ** (-8.0 * heads / heads))
    for i in range(heads - 2, -1, -1):
        s = jnp.where(h == i, jnp.float32(2.0 ** (-8.0 * (i + 1) / heads)), s)
    return s


def _df_lambda(lam_ref, lam_init):
    l = lam_ref[...]
    s1 = jnp.sum(l[0:1] * l[1:2], axis=1, keepdims=True)
    s2 = jnp.sum(l[2:3] * l[3:4], axis=1, keepdims=True)
    return jnp.exp(s1) - jnp.exp(s2) + lam_init


def _df_finish(acc0, l0, acc1, l1, lam, g, lam_init):
    o = acc0 / l0 - lam * (acc1 / l1)
    o = o * lax.rsqrt(jnp.mean(o * o, axis=-1, keepdims=True) + RMS_EPS)
    return o * g * (1.0 - lam_init)


def _df_prompt_kernel(q_ref, k_ref, v_ref, lam_ref, g_ref, o_ref, *, scale, lam_init, heads, dh):
    tq = q_ref.shape[0]
    dv = 2 * dh
    hp = q_ref.shape[1] // dv
    qi = pl.program_id(2)
    slopes = [_alibi_slope(pl.program_id(1) * hp + h, heads) for h in range(hp)]
    qs = [q_ref[:, n * dh:(n + 1) * dh].astype(BF16) for n in range(2 * hp)]
    row = lax.broadcasted_iota(jnp.int32, (tq, tq), 0)
    col = lax.broadcasted_iota(jnp.int32, (tq, tq), 1)
    rel = (row - col).astype(F32)

    def block(j, st):
        off = pl.multiple_of(j * tq, tq)
        dist = rel + ((qi - j) * tq).astype(F32)
        out = []
        for h in range(hp):
            vb = v_ref[pl.ds(off, tq), h * dv:(h + 1) * dv].astype(BF16)
            bias = slopes[h] * dist
            for c in range(2):
                n = 2 * h + c
                kb = k_ref[pl.ds(off, tq), n * dh:(n + 1) * dh].astype(BF16)
                s = _dot_nt(qs[n], kb) * scale - bias
                if st is None:
                    s = jnp.where(rel >= 0, s, -jnp.inf)
                m_blk = jnp.max(s, axis=1, keepdims=True)
                if st is None:
                    p = jnp.exp(s - m_blk)
                    out += [m_blk, jnp.sum(p, axis=1, keepdims=True), _dot(p.astype(BF16), vb)]
                else:
                    m, l, acc = st[3 * n:3 * n + 3]
                    m_new = jnp.maximum(m, m_blk)
                    alpha = jnp.exp(m - m_new)
                    p = jnp.exp(s - m_new)
                    out += [m_new, alpha * l + jnp.sum(p, axis=1, keepdims=True),
                            alpha * acc + _dot(p.astype(BF16), vb)]
        return tuple(out)

    st = lax.fori_loop(0, qi, lambda t, st: block(qi - 1 - t, st), block(qi, None))
    lam = _df_lambda(lam_ref, lam_init)
    g = g_ref[...]
    for h in range(hp):
        a0, a1 = st[6 * h:6 * h + 3], st[6 * h + 3:6 * h + 6]
        o_ref[:, h * dv:(h + 1) * dv] = _df_finish(a0[2], a0[1], a1[2], a1[1], lam, g, lam_init).astype(o_ref.dtype)


def _df_prompt(proj, lam_p, g, *, layer, batch, seq, heads, dh, q_col, k_col, v_col, lam_init):
    nq = seq // Q_TILE
    hp = math.gcd(heads, DF_HEADS_PER_STEP)
    w = hp * 2 * dh
    kern = functools.partial(_df_prompt_kernel, scale=dh ** -0.5, lam_init=lam_init, heads=heads, dh=dh)
    return pl.pallas_call(
        kern,
        grid=(batch, heads // hp, nq),
        in_specs=[
            pl.BlockSpec((Q_TILE, w), lambda b, h, i: (b * nq + i, q_col // w + h)),
            pl.BlockSpec((seq, w), lambda b, h, i: (b, k_col // w + h)),
            pl.BlockSpec((seq, w), lambda b, h, i: (b, v_col // w + h)),
            _layer_spec(lam_p, layer),
            _layer_spec(g, layer),
        ],
        out_specs=pl.BlockSpec((Q_TILE, w), lambda b, h, i: (b * nq + i, h)),
        out_shape=jax.ShapeDtypeStruct((batch * seq, heads * 2 * dh), BF16),
        compiler_params=_cparams(3),
        name="df_prompt",
    )(proj, proj, proj, lam_p, g)


def _decode_kernel(pt_ref, qkv_ref, sbk_hbm, sbv_hbm, dfk_hbm, dfv_hbm, u2_ref, lam_ref, g_ref,
                   osb_ref, odf_ref,
                   sbk_buf, sbv_buf, dfk_buf, dfv_buf, sems, sb_carry, sb_acc, df_m, df_l, df_acc,
                   *, layer, n_seq, n_pages, page, sb_heads, sb_dh, df_heads, df_dh, cols, lam_init):
    b = pl.program_id(0)
    n_chunks = n_pages // CHUNK_PAGES
    total_chunks = n_seq * n_chunks
    past_len = n_pages * page
    sb_q_col, sb_k_col, sb_v_col, df_q_col, df_k_col, df_v_col = cols
    sb_scale = sb_dh ** -0.5
    df_scale = df_dh ** -0.5
    dv = 2 * df_dh

    def chunk_copies(c, slot):
        seq = c // n_chunks
        first_page = n_pages - CHUNK_PAGES * (c % n_chunks + 1)
        copies = []
        for j in range(CHUNK_PAGES):
            phys = pt_ref[seq * n_pages + first_page + j]
            copies.append(pltpu.make_async_copy(sbk_hbm.at[layer, phys], sbk_buf.at[slot, j], sems.at[slot, 0]))
            copies.append(pltpu.make_async_copy(sbv_hbm.at[layer, phys], sbv_buf.at[slot, j], sems.at[slot, 1]))
            copies.append(pltpu.make_async_copy(dfk_hbm.at[layer, phys], dfk_buf.at[slot, j], sems.at[slot, 2]))
            copies.append(pltpu.make_async_copy(dfv_hbm.at[layer, phys], dfv_buf.at[slot, j], sems.at[slot, 3]))
        return copies

    def qslice(col, width):
        return qkv_ref[:, col:col + width].astype(BF16)

    def pad_rows(x):
        return jnp.concatenate([x, jnp.zeros((page - QPAD, x.shape[1]), x.dtype)], axis=0)

    def sb_update(ks, vs, kpos0, first):
        z = jnp.concatenate(
            [_dot_nt(qslice(sb_q_col + h * sb_dh, sb_dh), ks[h]) for h in range(sb_heads)],
            axis=0) * sb_scale
        valid = None
        if first:
            qpos = past_len + (lax.broadcasted_iota(jnp.int32, z.shape, 0) & (QPAD - 1))
            valid = kpos0 + lax.broadcasted_iota(jnp.int32, z.shape, 1) < qpos
        carry = jnp.zeros((z.shape[0], SCAN_W), F32) if first else sb_carry[...]
        a, carry = _sb_weights(z, u2_ref[...], carry, valid)
        a = a.astype(BF16)
        pv = jnp.concatenate(
            [_dot(a[h * QPAD:(h + 1) * QPAD], vs[h]) for h in range(sb_heads)], axis=0)
        sb_carry[...] = carry
        sb_acc[...] = pv if first else sb_acc[...] + pv

    def df_update(ks, vs, kpos0, first):
        s = jnp.concatenate(
            [_dot_nt(qslice(df_q_col + (2 * h + c) * df_dh, df_dh), ks[h][c])
             for h in range(df_heads) for c in range(2)], axis=0) * df_scale
        ridx = lax.broadcasted_iota(jnp.int32, s.shape, 0)
        qpos = past_len + (ridx & (QPAD - 1))
        kpos = kpos0 + lax.broadcasted_iota(jnp.int32, s.shape, 1)
        dist = (qpos - kpos).astype(F32)
        slope = _alibi_slope(ridx // (2 * QPAD), df_heads)
        s = s - slope * dist
        if first:
            s = jnp.where(dist >= 0, s, -jnp.inf)
        m_blk = jnp.max(s, axis=1, keepdims=True)
        m_new = m_blk if first else jnp.maximum(df_m[...], m_blk)
        p = jnp.exp(s - m_new)
        psum = jnp.sum(p, axis=1, keepdims=True)
        p = p.astype(BF16)
        pv = jnp.concatenate(
            [_dot(p[h * 2 * QPAD:(h + 1) * 2 * QPAD], vs[h]) for h in range(df_heads)], axis=0)
        if first:
            df_l[...] = psum
            df_acc[...] = pv
        else:
            alpha = jnp.exp(df_m[...] - m_new)
            df_l[...] = alpha * df_l[...] + psum
            df_acc[...] = alpha * df_acc[...] + pv
        df_m[...] = m_new

    ahead = N_SLOTS - 1

    @pl.when(b == 0)
    def _prime():
        for c in range(min(ahead, total_chunks)):
            for cp in chunk_copies(c, c % N_SLOTS):
                cp.start()

    ks = [pad_rows(qslice(sb_k_col + h * sb_dh, sb_dh)) for h in range(sb_heads)]
    vs = [pad_rows(qslice(sb_v_col + h * sb_dh, sb_dh)) for h in range(sb_heads)]
    sb_update(ks, vs, past_len, True)
    kd = [[pad_rows(qslice(df_k_col + (2 * h + c) * df_dh, df_dh)) for c in range(2)]
          for h in range(df_heads)]
    vd = [pad_rows(qslice(df_v_col + h * dv, dv)) for h in range(df_heads)]
    df_update(kd, vd, past_len, True)

    def cached_chunk(i, _):
        c = b * n_chunks + i
        slot = c % N_SLOTS

        @pl.when(c + ahead < total_chunks)
        def _prefetch():
            for cp in chunk_copies(c + ahead, (c + ahead) % N_SLOTS):
                cp.start()

        for cp in chunk_copies(c, slot):
            cp.wait()

        def head_rows(buf, off, stride):
            return jnp.concatenate([buf[slot, j, pl.ds(off, page, stride=stride), :]
                                    for j in range(CHUNK_PAGES)], axis=0).astype(BF16)

        kpos0 = (n_pages - CHUNK_PAGES * (i + 1)) * page
        ks = [head_rows(sbk_buf, h, sb_heads) for h in range(sb_heads)]
        vs = [head_rows(sbv_buf, h, sb_heads) for h in range(sb_heads)]
        sb_update(ks, vs, kpos0, False)
        kd = [[head_rows(dfk_buf, 2 * h + c, 2 * df_heads) for c in range(2)] for h in range(df_heads)]
        vd = [jnp.concatenate([head_rows(dfv_buf, e * df_heads + h, 2 * df_heads) for e in range(2)], axis=1)
              for h in range(df_heads)]
        df_update(kd, vd, kpos0, False)
        return 0

    lax.fori_loop(0, n_chunks, cached_chunk, 0)

    for h in range(sb_heads):
        osb_ref[:, h * sb_dh:(h + 1) * sb_dh] = sb_acc[h * QPAD:(h + 1) * QPAD, :].astype(osb_ref.dtype)
    lam = _df_lambda(lam_ref, lam_init)
    g = g_ref[...]
    for h in range(df_heads):
        r0 = h * 2 * QPAD
        o = _df_finish(df_acc[r0:r0 + QPAD, :], df_l[r0:r0 + QPAD, :],
                       df_acc[r0 + QPAD:r0 + 2 * QPAD, :], df_l[r0 + QPAD:r0 + 2 * QPAD, :],
                       lam, g, lam_init)
        odf_ref[:, h * dv:(h + 1) * dv] = o.astype(odf_ref.dtype)


def _decode_attn(qkv_new, page_table, cache_sb_k, cache_sb_v, cache_df_k, cache_df_v, u2, lam_p, g,
                 *, layer, cols, lam_init):
    nseq, _, width = qkv_new.shape
    n_pages = page_table.shape[1]
    _, _, page, sb_heads, sb_dh = cache_sb_k.shape
    df_heads, df_dh = cache_df_k.shape[3], cache_df_k.shape[5]
    dv = 2 * df_dh
    depth, n_phys = cache_sb_k.shape[:2]
    assert n_pages % CHUNK_PAGES == 0 and page == SCAN_W

    kern = functools.partial(_decode_kernel, layer=layer, n_seq=nseq, n_pages=n_pages, page=page,
                             sb_heads=sb_heads, sb_dh=sb_dh, df_heads=df_heads, df_dh=df_dh,
                             cols=cols, lam_init=lam_init)
    hbm = pl.BlockSpec(memory_space=pl.ANY)
    grid_spec = pltpu.PrefetchScalarGridSpec(
        num_scalar_prefetch=1,
        grid=(nseq,),
        in_specs=[
            pl.BlockSpec((None, QPAD, width), lambda b, pt: (b, 0, 0)),
            hbm, hbm, hbm, hbm,
            pl.BlockSpec(u2.shape, lambda b, pt: (0, 0)),
            _layer_spec(lam_p, layer),
            _layer_spec(g, layer),
        ],
        out_specs=[
            pl.BlockSpec((None, QPAD, sb_heads * sb_dh), lambda b, pt: (b, 0, 0)),
            pl.BlockSpec((None, QPAD, df_heads * dv), lambda b, pt: (b, 0, 0)),
        ],
        scratch_shapes=[
            pltpu.VMEM((N_SLOTS, CHUNK_PAGES, page * sb_heads, sb_dh), F32),
            pltpu.VMEM((N_SLOTS, CHUNK_PAGES, page * sb_heads, sb_dh), F32),
            pltpu.VMEM((N_SLOTS, CHUNK_PAGES, page * df_heads * 2, df_dh), F32),
            pltpu.VMEM((N_SLOTS, CHUNK_PAGES, page * 2 * df_heads, df_dh), F32),
            pltpu.SemaphoreType.DMA((N_SLOTS, 4)),
            pltpu.VMEM((sb_heads * QPAD, SCAN_W), F32),
            pltpu.VMEM((sb_heads * QPAD, sb_dh), F32),
            pltpu.VMEM((df_heads * 2 * QPAD, 1), F32),
            pltpu.VMEM((df_heads * 2 * QPAD, 1), F32),
            pltpu.VMEM((df_heads * 2 * QPAD, dv), F32),
        ],
    )
    df_v_rows = (cache_df_v.reshape(depth, n_phys, page, df_heads, 2, df_dh).swapaxes(3, 4)
                 .reshape(depth, n_phys, page * 2 * df_heads, df_dh))
    return pl.pallas_call(
        kern,
        grid_spec=grid_spec,
        out_shape=[jax.ShapeDtypeStruct((nseq, QPAD, sb_heads * sb_dh), BF16),
                   jax.ShapeDtypeStruct((nseq, QPAD, df_heads * dv), BF16)],
        compiler_params=_cparams(1),
        name="decode_attn",
    )(page_table.reshape(-1), qkv_new,
      cache_sb_k.reshape(depth, n_phys, page * sb_heads, sb_dh),
      cache_sb_v.reshape(depth, n_phys, page * sb_heads, sb_dh),
      cache_df_k.reshape(depth, n_phys, page * df_heads * 2, df_dh),
      df_v_rows, u2, lam_p, g)


def _conv_prompt_kernel(b_ref, c_ref, h_ref, w_ref, o_ref, hist_ref):
    u = c_ref[...] * h_ref[...]
    s = u.shape[0]
    row = lax.broadcasted_iota(jnp.int32, u.shape, 0)
    u1 = jnp.where(row >= 1, pltpu.roll(u, 1, axis=0), 0.0)
    u2 = jnp.where(row >= 2, pltpu.roll(u, 2, axis=0), 0.0)
    w = w_ref[...]
    conv = w[0:1] * u2 + w[1:2] * u1 + w[2:3] * u
    o_ref[...] = (b_ref[...] * conv).astype(o_ref.dtype)
    hist_ref[...] = u[s - 2:s, :]


def _conv_prompt(proj, conv_w, *, layer, batch, seq, width, b_col):
    nc = width // LANES
    cb = b_col // LANES
    return pl.pallas_call(
        _conv_prompt_kernel,
        grid=(batch, nc),
        in_specs=[
            pl.BlockSpec((seq, LANES), lambda b, c: (b, cb + c)),
            pl.BlockSpec((seq, LANES), lambda b, c: (b, cb + nc + c)),
            pl.BlockSpec((seq, LANES), lambda b, c: (b, cb + 2 * nc + c)),
            pl.BlockSpec((None, conv_w.shape[1], LANES), lambda b, c: (layer, 0, c)),
        ],
        out_specs=[pl.BlockSpec((seq, LANES), lambda b, c: (b, c)),
                   pl.BlockSpec((None, 2, LANES), lambda b, c: (b, 0, c))],
        out_shape=[jax.ShapeDtypeStruct((batch * seq, width), BF16),
                   jax.ShapeDtypeStruct((batch, 2, width), F32)],
        compiler_params=_cparams(2),
        name="conv_prompt",
    )(proj, proj, proj, conv_w)


def _conv_sample_kernel(b_ref, c_ref, h_ref, hist_ref, w_ref, o_ref, nh_ref, *, width, steps):
    u = c_ref[...] * h_ref[...]
    ext = jnp.concatenate([hist_ref[...], u], axis=1)
    w = w_ref[...]
    b = b_ref[...]
    for q in range(steps):
        conv = (w[0:1] * ext[:, q * width:(q + 1) * width]
                + w[1:2] * ext[:, (q + 1) * width:(q + 2) * width]
                + w[2:3] * ext[:, (q + 2) * width:(q + 3) * width])
        o_ref[:, q * width:(q + 1) * width] = (b[:, q * width:(q + 1) * width] * conv).astype(o_ref.dtype)
    nh_ref[...] = ext[:, steps * width:(steps + 2) * width]


def _conv_sample(cv_b, cv_c, cv_h, hist, conv_w, *, layer, width, steps):
    nseq = cv_b.shape[0]
    kern = functools.partial(_conv_sample_kernel, width=width, steps=steps)
    full = lambda a: pl.BlockSpec(a.shape, lambda i: (0,) * a.ndim)
    return pl.pallas_call(
        kern,
        grid=(1,),
        in_specs=[full(cv_b), full(cv_c), full(cv_h), _layer_spec(hist, layer), _layer_spec(conv_w, layer)],
        out_specs=[pl.BlockSpec((nseq, steps * width), lambda i: (0, 0)),
                   pl.BlockSpec((nseq, 2 * width), lambda i: (0, 0))],
        out_shape=[jax.ShapeDtypeStruct((nseq, steps * width), BF16),
                   jax.ShapeDtypeStruct((nseq, 2 * width), F32)],
        compiler_params=_cparams(1),
        name="conv_sample",
    )(cv_b, cv_c, cv_h, hist, conv_w)


def _merge_kernel(osb_ref, odf_ref, ocv_ref, gsb_ref, gdf_ref, gcv_ref, wsb_ref, wdf_ref, wcv_ref, o_ref):
    m = (jax.nn.sigmoid(gsb_ref[...]) * _dot(osb_ref[...], wsb_ref[...])
         + jax.nn.sigmoid(gdf_ref[...]) * _dot(odf_ref[...], wdf_ref[...])
         + jax.nn.sigmoid(gcv_ref[...]) * _dot(ocv_ref[...], wcv_ref[...]))
    o_ref[...] = m.astype(o_ref.dtype)


def _merge(o_sb, o_df, o_cv, proj, w_sb, w_df, w_cv, *, layer, gate_col, tm):
    t = o_sb.shape[0]
    d = w_sb.shape[2]
    gb = gate_col // d
    row = lambda w: pl.BlockSpec((tm, w), lambda i: (i, 0))
    return pl.pallas_call(
        _merge_kernel,
        grid=(t // tm,),
        in_specs=[row(o_sb.shape[1]), row(o_df.shape[1]), row(o_cv.shape[1]),
                  pl.BlockSpec((tm, d), lambda i: (i, gb)),
                  pl.BlockSpec((tm, d), lambda i: (i, gb + 1)),
                  pl.BlockSpec((tm, d), lambda i: (i, gb + 2)),
                  _layer_spec(w_sb, layer), _layer_spec(w_df, layer), _layer_spec(w_cv, layer)],
        out_specs=pl.BlockSpec((tm, d), lambda i: (i, 0)),
        out_shape=jax.ShapeDtypeStruct((t, d), BF16),
        compiler_params=_cparams(1),
        name="gated_merge",
    )(o_sb, o_df, o_cv, proj, proj, proj, w_sb, w_df, w_cv)


def _oproj_ln_kernel(m_ref, wo_ref, x_ref, g_ref, b_ref, xo_ref, xb_ref, *, alpha):
    y = alpha * x_ref[...] + _dot(m_ref[...], wo_ref[...])
    out = _layernorm(y, g_ref[...], b_ref[...])
    xo_ref[...] = out
    xb_ref[...] = out.astype(xb_ref.dtype)


def _oproj_ln(merged, w_o, x, g, b, *, layer, alpha, tm):
    t, d = x.shape
    row = pl.BlockSpec((tm, d), lambda i: (i, 0))
    return pl.pallas_call(
        functools.partial(_oproj_ln_kernel, alpha=alpha),
        grid=(t // tm,),
        in_specs=[row, _layer_spec(w_o, layer), row, _layer_spec(g, layer), _layer_spec(b, layer)],
        out_specs=[row, row],
        out_shape=[jax.ShapeDtypeStruct((t, d), F32), jax.ShapeDtypeStruct((t, d), BF16)],
        compiler_params=_cparams(1),
        name="oproj_ln",
    )(merged, w_o, x, g, b)


def _mlp_kernel(xb_ref, wu_ref, wd_ref, x_ref, g_ref, b_ref, xo_ref, xbo_ref, acc_ref, *, alpha):
    f = pl.program_id(1)
    h = jnp.maximum(_dot(xb_ref[...], wu_ref[...]), 0.0)
    part = _dot((h * h).astype(BF16), wd_ref[...])

    @pl.when(f == 0)
    def _():
        acc_ref[...] = part

    @pl.when(f > 0)
    def _():
        acc_ref[...] += part

    @pl.when(f == pl.num_programs(1) - 1)
    def _():
        out = _layernorm(alpha * x_ref[...] + acc_ref[...], g_ref[...], b_ref[...])
        xo_ref[...] = out
        xbo_ref[...] = out.astype(xbo_ref.dtype)


def _mlp_ln(xb, w_up, w_down, x, g, b, *, layer, alpha, tm, tf):
    t, d = x.shape
    ff = w_up.shape[2]
    row = pl.BlockSpec((tm, d), lambda i, f: (i, 0))
    return pl.pallas_call(
        functools.partial(_mlp_kernel, alpha=alpha),
        grid=(t // tm, ff // tf),
        in_specs=[row,
                  pl.BlockSpec((None, d, tf), lambda i, f: (layer, 0, f)),
                  pl.BlockSpec((None, tf, d), lambda i, f: (layer, f, 0)),
                  row, _layer_spec(g, layer), _layer_spec(b, layer)],
        out_specs=[row, row],
        out_shape=[jax.ShapeDtypeStruct((t, d), F32), jax.ShapeDtypeStruct((t, d), BF16)],
        scratch_shapes=[pltpu.VMEM((tm, d), F32)],
        compiler_params=_cparams(2),
        name="mlp_ln",
    )(xb, w_up, w_down, x, g, b)


def _kv_out_kernel(*refs, depth, sb_heads, sb_dh, df_heads, df_dh, n_df_blocks):
    per_layer = 2 + 2 * n_df_blocks
    o_sbk, o_sbv, o_dfk, o_dfv = refs[depth * per_layer:]
    tm, blk = refs[0].shape

    def emit(layer_refs):
        sbk_ref, sbv_ref = layer_refs[0], layer_refs[1]
        dfk_refs = layer_refs[2:2 + n_df_blocks]
        dfv_refs = layer_refs[2 + n_df_blocks:]

        def col(block_refs, c0, width):
            return block_refs[c0 // blk][:, c0 % blk:c0 % blk + width]

        for h in range(sb_heads):
            o_sbk[pl.ds(h, tm, stride=sb_heads), :] = sbk_ref[:, h * sb_dh:(h + 1) * sb_dh]
            o_sbv[pl.ds(h, tm, stride=sb_heads), :] = sbv_ref[:, h * sb_dh:(h + 1) * sb_dh]
        slots = 2 * df_heads
        for h in range(df_heads):
            for c in range(2):
                src = (2 * h + c) * df_dh
                o_dfk[pl.ds(2 * h + c, tm, stride=slots), :] = col(dfk_refs, src, df_dh)
                o_dfv[pl.ds(c * df_heads + h, tm, stride=slots), :] = col(dfv_refs, src, df_dh)

    for l in range(depth):
        pl.when(pl.program_id(0) == l)(
            functools.partial(emit, refs[l * per_layer:(l + 1) * per_layer]))


def _kv_out(projs, *, sb_heads, sb_dh, df_heads, df_dh, k_cols, tm):
    depth = len(projs)
    t = projs[0].shape[0]
    nt = t // tm
    sb_w = sb_heads * sb_dh
    df_w = df_heads * 2 * df_dh
    sb_k_col, sb_v_col, df_k_col, df_v_col = k_cols
    n_df_blocks = df_w // sb_w
    col0s = ([sb_k_col, sb_v_col] + [df_k_col + n * sb_w for n in range(n_df_blocks)]
             + [df_v_col + n * sb_w for n in range(n_df_blocks)])

    def blk(layer, c0):
        def index_map(l, i):
            return (jnp.where(l == layer, i, jnp.where(l < layer, 0, nt - 1)), c0 // sb_w)
        return pl.BlockSpec((tm, sb_w), index_map)

    in_specs = [blk(l, c0) for l in range(depth) for c0 in col0s]
    args = [p for p in projs for _ in col0s]
    rows = (sb_heads, sb_heads, 2 * df_heads, 2 * df_heads)
    widths = (sb_dh, sb_dh, df_dh, df_dh)
    kern = functools.partial(_kv_out_kernel, depth=depth, sb_heads=sb_heads, sb_dh=sb_dh,
                             df_heads=df_heads, df_dh=df_dh, n_df_blocks=n_df_blocks)
    return pl.pallas_call(
        kern,
        grid=(depth, nt),
        in_specs=in_specs,
        out_specs=[pl.BlockSpec((None, tm * r, w), lambda l, i: (l, i, 0)) for r, w in zip(rows, widths)],
        out_shape=[jax.ShapeDtypeStruct((depth, t * r, w), F32) for r, w in zip(rows, widths)],
        compiler_params=_cparams(2),
        name="kv_out",
    )(*args)


def _tile(n, pref):
    t = min(n, pref)
    while n % t:
        t //= 2
    return t


def kernel(x_prompt, x_sample, cache_sb_k, cache_sb_v, cache_df_k, cache_df_v, state_conv, page_table,
           w_in, conv_w, df_lambda, df_norm_g, w_br_sb, w_br_df, w_br_cv, w_o, ln1_g, ln1_b,
           w_up, w_down, ln2_g, ln2_b):
    batch, seq, d = x_prompt.shape
    nseq, steps, _ = x_sample.shape
    depth = w_in.shape[0]
    sb_heads, sb_dh = cache_sb_k.shape[3], cache_sb_k.shape[4]
    df_heads, df_dh = cache_df_k.shape[3], cache_df_k.shape[5]
    cv_w = state_conv.shape[3]
    sb_w = sb_heads * sb_dh
    df_w = df_heads * 2 * df_dh
    tp = batch * seq
    ts = nseq * steps
    alpha = (2 * depth) ** 0.25

    sb_q_col, sb_k_col, sb_v_col = 0, sb_w, 2 * sb_w
    df_q_col = 3 * sb_w
    df_k_col, df_v_col = df_q_col + df_w, df_q_col + 2 * df_w
    cv_col = df_q_col + 3 * df_w
    gate_col = cv_col + 3 * cv_w
    cols = (sb_q_col, sb_k_col, sb_v_col, df_q_col, df_k_col, df_v_col)

    u2 = _scan_matrix(SCAN_W)
    xs = [x_prompt.reshape(tp, d), x_sample.reshape(ts, d)]
    xbs = [x.astype(BF16) for x in xs]
    tms = [_tile(tp, 512), _tile(ts, 512)]
    kv_kw = dict(sb_heads=sb_heads, sb_dh=sb_dh, df_heads=df_heads, df_dh=df_dh,
                 k_cols=(sb_k_col, sb_v_col, df_k_col, df_v_col))

    g_df = df_norm_g.reshape(depth, 1, -1)
    w_br = [w.astype(BF16) for w in (w_br_sb, w_br_df, w_br_cv)]
    w_o_b, w_up_b, w_down_b = w_o.astype(BF16), w_up.astype(BF16), w_down.astype(BF16)
    ln = [a.reshape(depth, 1, d) for a in (ln1_g, ln1_b, ln2_g, ln2_b)]
    hist_in = state_conv.reshape(depth, nseq, 2 * cv_w)

    all_projs = [[], []]
    hists = [[], []]
    for l in range(depth):
        lam_init = 0.8 - 0.6 * math.exp(-0.3 * l)
        projs = [_matmul(xbs[s], w_in, layer=l, tm=_tile(xs[s].shape[0], 1024), tn=_tile(w_in.shape[2], 1024), name="in_proj")
                 for s in range(2)]
        for s in range(2):
            all_projs[s].append(projs[s])

        o_sb_p = _sb_prompt(projs[0], u2, batch=batch, seq=seq, heads=sb_heads, dh=sb_dh,
                            q_col=sb_q_col, k_col=sb_k_col, v_col=sb_v_col)
        o_df_p = _df_prompt(projs[0], df_lambda, g_df, layer=l, batch=batch, seq=seq, heads=df_heads,
                            dh=df_dh, q_col=df_q_col, k_col=df_k_col, v_col=df_v_col, lam_init=lam_init)
        o_cv_p, hist_p = _conv_prompt(projs[0], conv_w, layer=l, batch=batch, seq=seq, width=cv_w,
                                      b_col=cv_col)

        proj_s = projs[1]
        qkv_new = jnp.pad(proj_s[:, :cv_col].reshape(nseq, steps, cv_col),
                          ((0, 0), (0, QPAD - steps), (0, 0)))
        o_sb_s, o_df_s = _decode_attn(qkv_new, page_table, cache_sb_k, cache_sb_v, cache_df_k,
                                      cache_df_v, u2, df_lambda, g_df,
                                      layer=l, cols=cols, lam_init=lam_init)
        o_sb_s = o_sb_s[:, :steps].reshape(ts, sb_w)
        o_df_s = o_df_s[:, :steps].reshape(ts, df_w)
        cvs = [proj_s[:, cv_col + i * cv_w:cv_col + (i + 1) * cv_w].reshape(nseq, steps * cv_w)
               for i in range(3)]
        o_cv_s, hist_s = _conv_sample(cvs[0], cvs[1], cvs[2], hist_in, conv_w, layer=l, width=cv_w,
                                      steps=steps)
        hists[0].append(hist_p)
        hists[1].append(hist_s.reshape(nseq, 2, cv_w))

        branches = [(o_sb_p, o_df_p, o_cv_p), (o_sb_s, o_df_s, o_cv_s.reshape(ts, cv_w))]
        for s in range(2):
            merged = _merge(*branches[s], projs[s], *w_br, layer=l, gate_col=gate_col,
                            tm=_tile(xs[s].shape[0], 256))
            xs[s], xbs[s] = _oproj_ln(merged, w_o_b, xs[s], ln[0], ln[1], layer=l, alpha=alpha, tm=tms[s])
            xs[s], xbs[s] = _mlp_ln(xbs[s], w_up_b, w_down_b, xs[s], ln[2], ln[3], layer=l,
                                    alpha=alpha, tm=tms[s], tf=_tile(w_up.shape[2], 1024))

    def cache_entries(flat, lead):
        sbk, sbv, dfk, dfv = flat
        dfv = dfv.reshape(depth, *lead, 2, df_heads, df_dh).swapaxes(-3, -2)
        return (sbk.reshape(depth, *lead, sb_heads, sb_dh), sbv.reshape(depth, *lead, sb_heads, sb_dh),
                dfk.reshape(depth, *lead, df_heads, 2, df_dh), dfv.reshape(depth, *lead, df_heads, 2 * df_dh))

    kv_flat = [_kv_out(all_projs[s], tm=tms[s], **kv_kw) for s in range(2)]
    return (xs[0].reshape(batch, seq, d), xs[1].reshape(nseq, steps, d),
            *cache_entries(kv_flat[0], (batch, seq)), jnp.stack(hists[0], 0),
            *cache_entries(kv_flat[1], (nseq, steps)), jnp.stack(hists[1], 0))
```

```python
import functools
import math

import jax
import jax.numpy as jnp
from jax import lax
from jax.experimental import pallas as pl
from jax.experimental.pallas import tpu as pltpu

F32 = jnp.float32
BF16 = jnp.bfloat16

LN_EPS = 1e-5
RMS_EPS = 1e-6
LANES = 128
SUBLANES = 8
VMEM_LIMIT = 60 * 1024 * 1024
Q_TILE = 256
SCAN_W = 128
QPAD = 8
SB_HEADS_PER_STEP = 4
DF_HEADS_PER_STEP = 4
CHUNK_PAGES = 4
N_SLOTS = 3


def _cparams(n_axes):
    return pltpu.CompilerParams(dimension_semantics=("arbitrary",) * n_axes,
                                vmem_limit_bytes=VMEM_LIMIT)


def _layer_spec(a, layer):
    tail = (0,) * (a.ndim - 1)
    return pl.BlockSpec((None,) + a.shape[1:], lambda *_: (layer,) + tail)


def _dot(a, b):
    return jnp.dot(a, b, preferred_element_type=F32)


def _dot_nt(a, b):
    return lax.dot_general(a, b, (((1,), (1,)), ((), ())), preferred_element_type=F32)


def _log_sigmoid(z):
    return jnp.minimum(z, 0.0) - jnp.log(1.0 + jnp.exp(-jnp.abs(z)))


def _layernorm(y, g, b):
    mu = jnp.mean(y, axis=-1, keepdims=True)
    yc = y - mu
    var = jnp.mean(yc * yc, axis=-1, keepdims=True)
    return yc * lax.rsqrt(var + LN_EPS) * g + b


def _scan_matrix(tk):
    r = jnp.arange(tk)[:, None]
    c = jnp.arange(tk)[None, :]
    upper = (r > c).astype(BF16)
    half = jnp.concatenate([upper, jnp.ones((tk, tk), BF16)], axis=1)
    return jnp.concatenate([half, half], axis=0)


def _mm_kernel(x_ref, w_ref, o_ref, wb_ref):
    @pl.when(pl.program_id(1) == 0)
    def _():
        wb_ref[...] = w_ref[...].astype(BF16)

    o_ref[...] = _dot(x_ref[...], wb_ref[...]).astype(o_ref.dtype)


def _matmul(x, w, *, layer, tm, tn, name):
    m, k = x.shape
    n = w.shape[2]
    return pl.pallas_call(
        _mm_kernel,
        grid=(n // tn, m // tm),
        in_specs=[pl.BlockSpec((tm, k), lambda j, i: (i, 0)),
                  pl.BlockSpec((None, k, tn), lambda j, i: (layer, 0, j))],
        out_specs=pl.BlockSpec((tm, tn), lambda j, i: (i, j)),
        out_shape=jax.ShapeDtypeStruct((m, n), F32),
        scratch_shapes=[pltpu.VMEM((k, tn), BF16)],
        compiler_params=_cparams(2),
        name=name,
    )(x, w)


def _sb_weights(z, u2, carry, valid):
    w = u2.shape[0] // 2
    nseg = z.shape[1] // w
    ls = _log_sigmoid(z)
    lk = ls - z
    if valid is not None:
        lk = jnp.where(valid, lk, 0.0)
    hi = lk.astype(BF16)
    lo = (lk - hi.astype(F32)).astype(BF16)
    after = [None] * nseg
    for s in reversed(range(nseg)):
        seg = slice(s * w, (s + 1) * w)
        sc = _dot(jnp.concatenate([hi[:, seg], lo[:, seg]], axis=1), u2)
        after[s] = sc[:, :w] + carry
        carry = carry + sc[:, w:]
    a = jnp.exp(ls + (after[0] if nseg == 1 else jnp.concatenate(after, axis=1)))
    if valid is not None:
        a = jnp.where(valid, a, 0.0)
    return a, carry


def _sb_prompt_kernel(q_ref, k_ref, v_ref, u2_ref, o_ref, carry_ref, acc_ref, *, scale, dh):
    tq = q_ref.shape[0]
    hp = q_ref.shape[1] // dh
    qi = pl.program_id(2)
    heads = [slice(h * dh, (h + 1) * dh) for h in range(hp)]
    qs = [q_ref[:, hs].astype(BF16) for hs in heads]
    u2 = u2_ref[...]

    def block(j, diagonal):
        off = pl.multiple_of(j * tq, tq)
        valid = None
        if diagonal:
            valid = (lax.broadcasted_iota(jnp.int32, (tq, tq), 1)
                     < lax.broadcasted_iota(jnp.int32, (tq, tq), 0))
        for h, hs in enumerate(heads):
            kb = k_ref[pl.ds(off, tq), hs].astype(BF16)
            vb = v_ref[pl.ds(off, tq), hs].astype(BF16)
            carry = jnp.zeros((tq, SCAN_W), F32) if diagonal else carry_ref[h]
            a, carry = _sb_weights(_dot_nt(qs[h], kb) * scale, u2, carry, valid)
            pv = _dot(a.astype(BF16), vb)
            carry_ref[h] = carry
            acc_ref[h] = pv if diagonal else acc_ref[h] + pv

    block(qi, True)

    def body(t, _):
        block(qi - 1 - t, False)
        return 0

    lax.fori_loop(0, qi, body, 0)
    for h, hs in enumerate(heads):
        o_ref[:, hs] = acc_ref[h].astype(o_ref.dtype)


def _sb_prompt(proj, u2, *, batch, seq, heads, dh, q_col, k_col, v_col):
    nq = seq // Q_TILE
    hp = math.gcd(heads, SB_HEADS_PER_STEP)
    w = hp * dh
    kern = functools.partial(_sb_prompt_kernel, scale=dh ** -0.5, dh=dh)
    return pl.pallas_call(
        kern,
        grid=(batch, heads // hp, nq),
        in_specs=[
            pl.BlockSpec((Q_TILE, w), lambda b, h, i: (b * nq + i, q_col // w + h)),
            pl.BlockSpec((seq, w), lambda b, h, i: (b, k_col // w + h)),
            pl.BlockSpec((seq, w), lambda b, h, i: (b, v_col // w + h)),
            pl.BlockSpec(u2.shape, lambda b, h, i: (0, 0)),
        ],
        out_specs=pl.BlockSpec((Q_TILE, w), lambda b, h, i: (b * nq + i, h)),
        out_shape=jax.ShapeDtypeStruct((batch * seq, heads * dh), BF16),
        scratch_shapes=[pltpu.VMEM((hp, Q_TILE, SCAN_W), F32),
                        pltpu.VMEM((hp, Q_TILE, dh), F32)],
        compiler_params=_cparams(3),
        name="sb_prompt",
    )(proj, proj, proj, u2)


def _alibi_slope(h, heads):
    s = jnp.float32(2.0 ** (-8.0 * heads / heads))
    for i in range(heads - 2, -1, -1):
        s = jnp.where(h == i, jnp.float32(2.0 ** (-8.0 * (i + 1) / heads)), s)
    return s


def _df_lambda(lam_ref, lam_init):
    l = lam_ref[...]
    s1 = jnp.sum(l[0:1] * l[1:2], axis=1, keepdims=True)
    s2 = jnp.sum(l[2:3] * l[3:4], axis=1, keepdims=True)
    return jnp.exp(s1) - jnp.exp(s2) + lam_init


def _df_finish(acc0, l0, acc1, l1, lam, g, lam_init):
    o = acc0 / l0 - lam * (acc1 / l1)
    o = o * lax.rsqrt(jnp.mean(o * o, axis=-1, keepdims=True) + RMS_EPS)
    return o * g * (1.0 - lam_init)


def _df_prompt_kernel(*refs, groups, scale, lam_init, heads, dh):
    q_refs, k_refs, v_refs = refs[:groups], refs[groups:2 * groups], refs[2 * groups:3 * groups]
    lam_ref, g_ref, o_ref, kb_ref, vb_ref, p_ref = refs[3 * groups:]
    tq, gw = q_refs[0].shape
    dv = 2 * dh
    hp = groups * gw // dv
    nq = k_refs[0].shape[0] // tq
    qi = pl.program_id(2)
    units = range(2 * hp)

    @pl.when(qi == 0)
    def _():
        for gi in range(groups):
            kb_ref[:, gi * gw:(gi + 1) * gw] = k_refs[gi][...].astype(BF16)
            vb_ref[:, gi * gw:(gi + 1) * gw] = v_refs[gi][...].astype(BF16)

    slopes = [_alibi_slope(pl.program_id(1) * hp + h, heads) for h in range(hp)]
    qs = [q_refs[n * dh // gw][:, n * dh % gw:n * dh % gw + dh].astype(BF16) for n in units]
    row = lax.broadcasted_iota(jnp.int32, (tq, tq), 0)
    col = lax.broadcasted_iota(jnp.int32, (tq, tq), 1)
    rel = (row - col).astype(F32)

    def scores(j, n, diagonal):
        off = pl.multiple_of(j * tq, tq)
        dist = rel + ((qi - j) * tq).astype(F32)
        s = _dot_nt(qs[n], kb_ref[pl.ds(off, tq), n * dh:(n + 1) * dh]) * scale - slopes[n // 2] * dist
        return jnp.where(rel >= 0, s, -jnp.inf) if diagonal else s

    def row_max(j, n, diagonal):
        return jnp.max(scores(j, n, diagonal), axis=1, keepdims=True)

    m = tuple(row_max(qi, n, True) for n in units)
    m = lax.fori_loop(0, qi, lambda j, m: tuple(jnp.maximum(m[n], row_max(j, n, False)) for n in units), m)

    def weights(j, n, diagonal):
        p = jnp.exp(scores(j, n, diagonal) - m[n])
        p_ref[n, j] = p.astype(BF16)
        return jnp.sum(p, axis=1, keepdims=True)

    l = tuple(weights(qi, n, True) for n in units)
    l = lax.fori_loop(0, qi, lambda j, l: tuple(l[n] + weights(j, n, False) for n in units), l)

    lam = _df_lambda(lam_ref, lam_init)
    g = g_ref[...]

    def finish(n_blocks):
        keys = n_blocks * tq
        for h in range(hp):
            vb = vb_ref[0:keys, h * dv:(h + 1) * dv]
            acc = [_dot(jnp.concatenate([p_ref[2 * h + c, jj] for jj in range(n_blocks)], axis=1), vb)
                   for c in range(2)]
            o = _df_finish(acc[0], l[2 * h], acc[1], l[2 * h + 1], lam, g, lam_init)
            o_ref[:, h * dv:(h + 1) * dv] = o.astype(o_ref.dtype)

    for k in range(nq):
        pl.when(qi == k)(functools.partial(finish, k + 1))


def _df_prompt(proj, lam_p, g, *, layer, batch, seq, heads, dh, q_col, k_col, v_col, lam_init):
    nq = seq // Q_TILE
    hp = math.gcd(heads, DF_HEADS_PER_STEP)
    w = hp * 2 * dh
    gw = math.gcd(math.gcd(w, q_col), math.gcd(k_col, v_col))
    groups = w // gw

    def col_blocks(rows, col, row_map):
        return [pl.BlockSpec((rows, gw), lambda b, h, i, n=n: (row_map(b, i), col // gw + h * groups + n))
                for n in range(groups)]

    kern = functools.partial(_df_prompt_kernel, groups=groups, scale=dh ** -0.5, lam_init=lam_init,
                             heads=heads, dh=dh)
    return pl.pallas_call(
        kern,
        grid=(batch, heads // hp, nq),
        in_specs=(col_blocks(Q_TILE, q_col, lambda b, i: b * nq + i)
                  + col_blocks(seq, k_col, lambda b, i: b)
                  + col_blocks(seq, v_col, lambda b, i: b)
                  + [_layer_spec(lam_p, layer), _layer_spec(g, layer)]),
        out_specs=pl.BlockSpec((Q_TILE, w), lambda b, h, i: (b * nq + i, h)),
        out_shape=jax.ShapeDtypeStruct((batch * seq, heads * 2 * dh), BF16),
        scratch_shapes=[pltpu.VMEM((seq, w), BF16),
                        pltpu.VMEM((seq, w), BF16),
                        pltpu.VMEM((2 * hp, nq, Q_TILE, Q_TILE), BF16)],
        compiler_params=_cparams(3),
        name="df_prompt",
    )(*([proj] * (3 * groups)), lam_p, g)


def _decode_kernel(pt_ref, qkv_ref, sbk_hbm, sbv_hbm, dfk_hbm, dfv_hbm, u2_ref, lam_ref, g_ref,
                   osb_ref, odf_ref,
                   sbk_buf, sbv_buf, dfk_buf, dfv_buf, sems, sb_carry, sb_acc, df_m, df_l, df_acc,
                   *, layer, n_seq, n_pages, page, sb_heads, sb_dh, df_heads, df_dh, cols, lam_init):
    b = pl.program_id(0)
    n_chunks = n_pages // CHUNK_PAGES
    total_chunks = n_seq * n_chunks
    past_len = n_pages * page
    sb_q_col, sb_k_col, sb_v_col, df_q_col, df_k_col, df_v_col = cols
    sb_scale = sb_dh ** -0.5
    df_scale = df_dh ** -0.5
    dv = 2 * df_dh

    def chunk_copies(c, slot):
        seq = c // n_chunks
        first_page = n_pages - CHUNK_PAGES * (c % n_chunks + 1)
        copies = []
        for j in range(CHUNK_PAGES):
            phys = pt_ref[seq * n_pages + first_page + j]
            copies.append(pltpu.make_async_copy(sbk_hbm.at[layer, phys], sbk_buf.at[slot, j], sems.at[slot, 0]))
            copies.append(pltpu.make_async_copy(sbv_hbm.at[layer, phys], sbv_buf.at[slot, j], sems.at[slot, 1]))
            copies.append(pltpu.make_async_copy(dfk_hbm.at[layer, phys], dfk_buf.at[slot, j], sems.at[slot, 2]))
            copies.append(pltpu.make_async_copy(dfv_hbm.at[layer, phys], dfv_buf.at[slot, j], sems.at[slot, 3]))
        return copies

    def qslice(col, width):
        return qkv_ref[:, col:col + width].astype(BF16)

    def pad_rows(x):
        return jnp.concatenate([x, jnp.zeros((page - QPAD, x.shape[1]), x.dtype)], axis=0)

    def sb_update(ks, vs, kpos0, first):
        z = jnp.concatenate(
            [_dot_nt(qslice(sb_q_col + h * sb_dh, sb_dh), ks[h]) for h in range(sb_heads)],
            axis=0) * sb_scale
        valid = None
        if first:
            qpos = past_len + (lax.broadcasted_iota(jnp.int32, z.shape, 0) & (QPAD - 1))
            valid = kpos0 + lax.broadcasted_iota(jnp.int32, z.shape, 1) < qpos
        carry = jnp.zeros((z.shape[0], SCAN_W), F32) if first else sb_carry[...]
        a, carry = _sb_weights(z, u2_ref[...], carry, valid)
        a = a.astype(BF16)
        pv = jnp.concatenate(
            [_dot(a[h * QPAD:(h + 1) * QPAD], vs[h]) for h in range(sb_heads)], axis=0)
        sb_carry[...] = carry
        sb_acc[...] = pv if first else sb_acc[...] + pv

    def df_update(ks, vs, kpos0, first):
        s = jnp.concatenate(
            [_dot_nt(qslice(df_q_col + (2 * h + c) * df_dh, df_dh), ks[h][c])
             for h in range(df_heads) for c in range(2)], axis=0) * df_scale
        ridx = lax.broadcasted_iota(jnp.int32, s.shape, 0)
        qpos = past_len + (ridx & (QPAD - 1))
        kpos = kpos0 + lax.broadcasted_iota(jnp.int32, s.shape, 1)
        dist = (qpos - kpos).astype(F32)
        slope = _alibi_slope(ridx // (2 * QPAD), df_heads)
        s = s - slope * dist
        if first:
            s = jnp.where(dist >= 0, s, -jnp.inf)
        m_blk = jnp.max(s, axis=1, keepdims=True)
        m_new = m_blk if first else jnp.maximum(df_m[...], m_blk)
        p = jnp.exp(s - m_new)
        psum = jnp.sum(p, axis=1, keepdims=True)
        p = p.astype(BF16)
        pv = jnp.concatenate(
            [_dot(p[h * 2 * QPAD:(h + 1) * 2 * QPAD], vs[h]) for h in range(df_heads)], axis=0)
        if first:
            df_l[...] = psum
            df_acc[...] = pv
        else:
            alpha = jnp.exp(df_m[...] - m_new)
            df_l[...] = alpha * df_l[...] + psum
            df_acc[...] = alpha * df_acc[...] + pv
        df_m[...] = m_new

    ahead = N_SLOTS - 1

    @pl.when(b == 0)
    def _prime():
        for c in range(min(ahead, total_chunks)):
            for cp in chunk_copies(c, c % N_SLOTS):
                cp.start()

    ks = [pad_rows(qslice(sb_k_col + h * sb_dh, sb_dh)) for h in range(sb_heads)]
    vs = [pad_rows(qslice(sb_v_col + h * sb_dh, sb_dh)) for h in range(sb_heads)]
    sb_update(ks, vs, past_len, True)
    kd = [[pad_rows(qslice(df_k_col + (2 * h + c) * df_dh, df_dh)) for c in range(2)]
          for h in range(df_heads)]
    vd = [pad_rows(qslice(df_v_col + h * dv, dv)) for h in range(df_heads)]
    df_update(kd, vd, past_len, True)

    def cached_chunk(i, _):
        c = b * n_chunks + i
        slot = c % N_SLOTS

        @pl.when(c + ahead < total_chunks)
        def _prefetch():
            for cp in chunk_copies(c + ahead, (c + ahead) % N_SLOTS):
                cp.start()

        for cp in chunk_copies(c, slot):
            cp.wait()

        def head_rows(buf, off, stride):
            return jnp.concatenate([buf[slot, j, pl.ds(off, page, stride=stride), :]
                                    for j in range(CHUNK_PAGES)], axis=0).astype(BF16)

        kpos0 = (n_pages - CHUNK_PAGES * (i + 1)) * page
        ks = [head_rows(sbk_buf, h, sb_heads) for h in range(sb_heads)]
        vs = [head_rows(sbv_buf, h, sb_heads) for h in range(sb_heads)]
        sb_update(ks, vs, kpos0, False)
        kd = [[head_rows(dfk_buf, 2 * h + c, 2 * df_heads) for c in range(2)] for h in range(df_heads)]
        vd = [jnp.concatenate([head_rows(dfv_buf, e * df_heads + h, 2 * df_heads) for e in range(2)], axis=1)
              for h in range(df_heads)]
        df_update(kd, vd, kpos0, False)
        return 0

    lax.fori_loop(0, n_chunks, cached_chunk, 0)

    for h in range(sb_heads):
        osb_ref[:, h * sb_dh:(h + 1) * sb_dh] = sb_acc[h * QPAD:(h + 1) * QPAD, :].astype(osb_ref.dtype)
    lam = _df_lambda(lam_ref, lam_init)
    g = g_ref[...]
    for h in range(df_heads):
        r0 = h * 2 * QPAD
        o = _df_finish(df_acc[r0:r0 + QPAD, :], df_l[r0:r0 + QPAD, :],
                       df_acc[r0 + QPAD:r0 + 2 * QPAD, :], df_l[r0 + QPAD:r0 + 2 * QPAD, :],
                       lam, g, lam_init)
        odf_ref[:, h * dv:(h + 1) * dv] = o.astype(odf_ref.dtype)


def _decode_attn(qkv_new, page_table, cache_sb_k, cache_sb_v, cache_df_k, cache_df_v, u2, lam_p, g,
                 *, layer, cols, lam_init):
    nseq, _, width = qkv_new.shape
    n_pages = page_table.shape[1]
    _, _, page, sb_heads, sb_dh = cache_sb_k.shape
    df_heads, df_dh = cache_df_k.shape[3], cache_df_k.shape[5]
    dv = 2 * df_dh
    depth, n_phys = cache_sb_k.shape[:2]
    assert n_pages % CHUNK_PAGES == 0 and page == SCAN_W

    kern = functools.partial(_decode_kernel, layer=layer, n_seq=nseq, n_pages=n_pages, page=page,
                             sb_heads=sb_heads, sb_dh=sb_dh, df_heads=df_heads, df_dh=df_dh,
                             cols=cols, lam_init=lam_init)
    hbm = pl.BlockSpec(memory_space=pl.ANY)
    grid_spec = pltpu.PrefetchScalarGridSpec(
        num_scalar_prefetch=1,
        grid=(nseq,),
        in_specs=[
            pl.BlockSpec((None, QPAD, width), lambda b, pt: (b, 0, 0)),
            hbm, hbm, hbm, hbm,
            pl.BlockSpec(u2.shape, lambda b, pt: (0, 0)),
            _layer_spec(lam_p, layer),
            _layer_spec(g, layer),
        ],
        out_specs=[
            pl.BlockSpec((None, QPAD, sb_heads * sb_dh), lambda b, pt: (b, 0, 0)),
            pl.BlockSpec((None, QPAD, df_heads * dv), lambda b, pt: (b, 0, 0)),
        ],
        scratch_shapes=[
            pltpu.VMEM((N_SLOTS, CHUNK_PAGES, page * sb_heads, sb_dh), F32),
            pltpu.VMEM((N_SLOTS, CHUNK_PAGES, page * sb_heads, sb_dh), F32),
            pltpu.VMEM((N_SLOTS, CHUNK_PAGES, page * df_heads * 2, df_dh), F32),
            pltpu.VMEM((N_SLOTS, CHUNK_PAGES, page * 2 * df_heads, df_dh), F32),
            pltpu.SemaphoreType.DMA((N_SLOTS, 4)),
            pltpu.VMEM((sb_heads * QPAD, SCAN_W), F32),
            pltpu.VMEM((sb_heads * QPAD, sb_dh), F32),
            pltpu.VMEM((df_heads * 2 * QPAD, 1), F32),
            pltpu.VMEM((df_heads * 2 * QPAD, 1), F32),
            pltpu.VMEM((df_heads * 2 * QPAD, dv), F32),
        ],
    )
    df_v_rows = (cache_df_v.reshape(depth, n_phys, page, df_heads, 2, df_dh).swapaxes(3, 4)
                 .reshape(depth, n_phys, page * 2 * df_heads, df_dh))
    return pl.pallas_call(
        kern,
        grid_spec=grid_spec,
        out_shape=[jax.ShapeDtypeStruct((nseq, QPAD, sb_heads * sb_dh), BF16),
                   jax.ShapeDtypeStruct((nseq, QPAD, df_heads * dv), BF16)],
        compiler_params=_cparams(1),
        name="decode_attn",
    )(page_table.reshape(-1), qkv_new,
      cache_sb_k.reshape(depth, n_phys, page * sb_heads, sb_dh),
      cache_sb_v.reshape(depth, n_phys, page * sb_heads, sb_dh),
      cache_df_k.reshape(depth, n_phys, page * df_heads * 2, df_dh),
      df_v_rows, u2, lam_p, g)


def _conv_prompt_kernel(b_ref, c_ref, h_ref, w_ref, o_ref, hist_ref):
    u = c_ref[...] * h_ref[...]
    s = u.shape[0]
    row = lax.broadcasted_iota(jnp.int32, u.shape, 0)
    u1 = jnp.where(row >= 1, pltpu.roll(u, 1, axis=0), 0.0)
    u2 = jnp.where(row >= 2, pltpu.roll(u, 2, axis=0), 0.0)
    w = w_ref[...]
    conv = w[0:1] * u2 + w[1:2] * u1 + w[2:3] * u
    o_ref[...] = (b_ref[...] * conv).astype(o_ref.dtype)
    hist_ref[...] = u[s - 2:s, :]


def _conv_prompt(proj, conv_w, *, layer, batch, seq, width, b_col):
    nc = width // LANES
    cb = b_col // LANES
    return pl.pallas_call(
        _conv_prompt_kernel,
        grid=(batch, nc),
        in_specs=[
            pl.BlockSpec((seq, LANES), lambda b, c: (b, cb + c)),
            pl.BlockSpec((seq, LANES), lambda b, c: (b, cb + nc + c)),
            pl.BlockSpec((seq, LANES), lambda b, c: (b, cb + 2 * nc + c)),
            pl.BlockSpec((None, conv_w.shape[1], LANES), lambda b, c: (layer, 0, c)),
        ],
        out_specs=[pl.BlockSpec((seq, LANES), lambda b, c: (b, c)),
                   pl.BlockSpec((None, 2, LANES), lambda b, c: (b, 0, c))],
        out_shape=[jax.ShapeDtypeStruct((batch * seq, width), BF16),
                   jax.ShapeDtypeStruct((batch, 2, width), F32)],
        compiler_params=_cparams(2),
        name="conv_prompt",
    )(proj, proj, proj, conv_w)


def _conv_sample_kernel(b_ref, c_ref, h_ref, hist_ref, w_ref, o_ref, nh_ref, *, width, steps):
    u = c_ref[...] * h_ref[...]
    ext = jnp.concatenate([hist_ref[...], u], axis=1)
    w = w_ref[...]
    b = b_ref[...]
    for q in range(steps):
        conv = (w[0:1] * ext[:, q * width:(q + 1) * width]
                + w[1:2] * ext[:, (q + 1) * width:(q + 2) * width]
                + w[2:3] * ext[:, (q + 2) * width:(q + 3) * width])
        o_ref[:, q * width:(q + 1) * width] = (b[:, q * width:(q + 1) * width] * conv).astype(o_ref.dtype)
    nh_ref[...] = ext[:, steps * width:(steps + 2) * width]


def _conv_sample(cv_b, cv_c, cv_h, hist, conv_w, *, layer, width, steps):
    nseq = cv_b.shape[0]
    kern = functools.partial(_conv_sample_kernel, width=width, steps=steps)
    full = lambda a: pl.BlockSpec(a.shape, lambda i: (0,) * a.ndim)
    return pl.pallas_call(
        kern,
        grid=(1,),
        in_specs=[full(cv_b), full(cv_c), full(cv_h), _layer_spec(hist, layer), _layer_spec(conv_w, layer)],
        out_specs=[pl.BlockSpec((nseq, steps * width), lambda i: (0, 0)),
                   pl.BlockSpec((nseq, 2 * width), lambda i: (0, 0))],
        out_shape=[jax.ShapeDtypeStruct((nseq, steps * width), BF16),
                   jax.ShapeDtypeStruct((nseq, 2 * width), F32)],
        compiler_params=_cparams(1),
        name="conv_sample",
    )(cv_b, cv_c, cv_h, hist, conv_w)


def _merge_kernel(osb_ref, odf_ref, ocv_ref, gsb_ref, gdf_ref, gcv_ref, wsb_ref, wdf_ref, wcv_ref, o_ref):
    m = (jax.nn.sigmoid(gsb_ref[...]) * _dot(osb_ref[...], wsb_ref[...])
         + jax.nn.sigmoid(gdf_ref[...]) * _dot(odf_ref[...], wdf_ref[...])
         + jax.nn.sigmoid(gcv_ref[...]) * _dot(ocv_ref[...], wcv_ref[...]))
    o_ref[...] = m.astype(o_ref.dtype)


def _merge(o_sb, o_df, o_cv, proj, w_sb, w_df, w_cv, *, layer, gate_col, tm):
    t = o_sb.shape[0]
    d = w_sb.shape[2]
    gb = gate_col // d
    row = lambda w: pl.BlockSpec((tm, w), lambda i: (i, 0))
    return pl.pallas_call(
        _merge_kernel,
        grid=(t // tm,),
        in_specs=[row(o_sb.shape[1]), row(o_df.shape[1]), row(o_cv.shape[1]),
                  pl.BlockSpec((tm, d), lambda i: (i, gb)),
                  pl.BlockSpec((tm, d), lambda i: (i, gb + 1)),
                  pl.BlockSpec((tm, d), lambda i: (i, gb + 2)),
                  _layer_spec(w_sb, layer), _layer_spec(w_df, layer), _layer_spec(w_cv, layer)],
        out_specs=pl.BlockSpec((tm, d), lambda i: (i, 0)),
        out_shape=jax.ShapeDtypeStruct((t, d), BF16),
        compiler_params=_cparams(1),
        name="gated_merge",
    )(o_sb, o_df, o_cv, proj, proj, proj, w_sb, w_df, w_cv)


def _oproj_ln_kernel(m_ref, wo_ref, x_ref, g_ref, b_ref, xo_ref, xb_ref, *, alpha):
    y = alpha * x_ref[...] + _dot(m_ref[...], wo_ref[...])
    out = _layernorm(y, g_ref[...], b_ref[...])
    xo_ref[...] = out
    xb_ref[...] = out.astype(xb_ref.dtype)


def _oproj_ln(merged, w_o, x, g, b, *, layer, alpha, tm):
    t, d = x.shape
    row = pl.BlockSpec((tm, d), lambda i: (i, 0))
    return pl.pallas_call(
        functools.partial(_oproj_ln_kernel, alpha=alpha),
        grid=(t // tm,),
        in_specs=[row, _layer_spec(w_o, layer), row, _layer_spec(g, layer), _layer_spec(b, layer)],
        out_specs=[row, row],
        out_shape=[jax.ShapeDtypeStruct((t, d), F32), jax.ShapeDtypeStruct((t, d), BF16)],
        compiler_params=_cparams(1),
        name="oproj_ln",
    )(merged, w_o, x, g, b)


def _mlp_kernel(xb_ref, wu_ref, wd_ref, x_ref, g_ref, b_ref, xo_ref, xbo_ref, acc_ref, *, alpha):
    f = pl.program_id(1)

    @pl.when(f == 0)
    def _():
        acc_ref[...] = jnp.zeros_like(acc_ref)

    h = jnp.maximum(_dot(xb_ref[...], wu_ref[...]), 0.0)
    acc_ref[...] += _dot((h * h).astype(BF16), wd_ref[...])

    @pl.when(f == pl.num_programs(1) - 1)
    def _():
        out = _layernorm(alpha * x_ref[...] + acc_ref[...], g_ref[...], b_ref[...])
        xo_ref[...] = out
        xbo_ref[...] = out.astype(xbo_ref.dtype)


def _mlp_ln(xb, w_up, w_down, x, g, b, *, layer, alpha, tm, tf):
    t, d = x.shape
    ff = w_up.shape[2]
    row = pl.BlockSpec((tm, d), lambda i, f: (i, 0))
    return pl.pallas_call(
        functools.partial(_mlp_kernel, alpha=alpha),
        grid=(t // tm, ff // tf),
        in_specs=[row,
                  pl.BlockSpec((None, d, tf), lambda i, f: (layer, 0, f)),
                  pl.BlockSpec((None, tf, d), lambda i, f: (layer, f, 0)),
                  row, _layer_spec(g, layer), _layer_spec(b, layer)],
        out_specs=[row, row],
        out_shape=[jax.ShapeDtypeStruct((t, d), F32), jax.ShapeDtypeStruct((t, d), BF16)],
        scratch_shapes=[pltpu.VMEM((tm, d), F32)],
        compiler_params=_cparams(2),
        name="mlp_ln",
    )(xb, w_up, w_down, x, g, b)


def _kv_out_kernel(*refs, depth, sb_heads, sb_dh, df_heads, df_dh, n_df_blocks):
    per_layer = 2 + 2 * n_df_blocks
    o_sbk, o_sbv, o_dfk, o_dfv = refs[depth * per_layer:]
    tm, blk = refs[0].shape

    def emit(layer_refs):
        sbk_ref, sbv_ref = layer_refs[0], layer_refs[1]
        dfk_refs = layer_refs[2:2 + n_df_blocks]
        dfv_refs = layer_refs[2 + n_df_blocks:]

        def col(block_refs, c0, width):
            return block_refs[c0 // blk][:, c0 % blk:c0 % blk + width]

        for h in range(sb_heads):
            o_sbk[pl.ds(h, tm, stride=sb_heads), :] = sbk_ref[:, h * sb_dh:(h + 1) * sb_dh]
            o_sbv[pl.ds(h, tm, stride=sb_heads), :] = sbv_ref[:, h * sb_dh:(h + 1) * sb_dh]
        slots = 2 * df_heads
        for h in range(df_heads):
            for c in range(2):
                src = (2 * h + c) * df_dh
                o_dfk[pl.ds(2 * h + c, tm, stride=slots), :] = col(dfk_refs, src, df_dh)
                o_dfv[pl.ds(c * df_heads + h, tm, stride=slots), :] = col(dfv_refs, src, df_dh)

    for l in range(depth):
        pl.when(pl.program_id(0) == l)(
            functools.partial(emit, refs[l * per_layer:(l + 1) * per_layer]))


def _kv_out(projs, *, sb_heads, sb_dh, df_heads, df_dh, k_cols, tm):
    depth = len(projs)
    t = projs[0].shape[0]
    nt = t // tm
    sb_w = sb_heads * sb_dh
    df_w = df_heads * 2 * df_dh
    sb_k_col, sb_v_col, df_k_col, df_v_col = k_cols
    n_df_blocks = df_w // sb_w
    col0s = ([sb_k_col, sb_v_col] + [df_k_col + n * sb_w for n in range(n_df_blocks)]
             + [df_v_col + n * sb_w for n in range(n_df_blocks)])

    def blk(layer, c0):
        def index_map(l, i):
            return (jnp.where(l == layer, i, jnp.where(l < layer, 0, nt - 1)), c0 // sb_w)
        return pl.BlockSpec((tm, sb_w), index_map)

    in_specs = [blk(l, c0) for l in range(depth) for c0 in col0s]
    args = [p for p in projs for _ in col0s]
    rows = (sb_heads, sb_heads, 2 * df_heads, 2 * df_heads)
    widths = (sb_dh, sb_dh, df_dh, df_dh)
    kern = functools.partial(_kv_out_kernel, depth=depth, sb_heads=sb_heads, sb_dh=sb_dh,
                             df_heads=df_heads, df_dh=df_dh, n_df_blocks=n_df_blocks)
    return pl.pallas_call(
        kern,
        grid=(depth, nt),
        in_specs=in_specs,
        out_specs=[pl.BlockSpec((None, tm * r, w), lambda l, i: (l, i, 0)) for r, w in zip(rows, widths)],
        out_shape=[jax.ShapeDtypeStruct((depth, t * r, w), F32) for r, w in zip(rows, widths)],
        compiler_params=_cparams(2),
        name="kv_out",
    )(*args)


def _tile(n, pref):
    t = min(n, pref)
    while n % t:
        t //= 2
    return t


def kernel(x_prompt, x_sample, cache_sb_k, cache_sb_v, cache_df_k, cache_df_v, state_conv, page_table,
           w_in, conv_w, df_lambda, df_norm_g, w_br_sb, w_br_df, w_br_cv, w_o, ln1_g, ln1_b,
           w_up, w_down, ln2_g, ln2_b):
    batch, seq, d = x_prompt.shape
    nseq, steps, _ = x_sample.shape
    depth = w_in.shape[0]
    sb_heads, sb_dh = cache_sb_k.shape[3], cache_sb_k.shape[4]
    df_heads, df_dh = cache_df_k.shape[3], cache_df_k.shape[5]
    cv_w = state_conv.shape[3]
    sb_w = sb_heads * sb_dh
    df_w = df_heads * 2 * df_dh
    tp = batch * seq
    ts = nseq * steps
    alpha = (2 * depth) ** 0.25

    sb_q_col, sb_k_col, sb_v_col = 0, sb_w, 2 * sb_w
    df_q_col = 3 * sb_w
    df_k_col, df_v_col = df_q_col + df_w, df_q_col + 2 * df_w
    cv_col = df_q_col + 3 * df_w
    gate_col = cv_col + 3 * cv_w
    cols = (sb_q_col, sb_k_col, sb_v_col, df_q_col, df_k_col, df_v_col)

    u2 = _scan_matrix(SCAN_W)
    xs = [x_prompt.reshape(tp, d), x_sample.reshape(ts, d)]
    xbs = [x.astype(BF16) for x in xs]
    tms = [_tile(tp, 512), _tile(ts, 512)]
    kv_kw = dict(sb_heads=sb_heads, sb_dh=sb_dh, df_heads=df_heads, df_dh=df_dh,
                 k_cols=(sb_k_col, sb_v_col, df_k_col, df_v_col))

    g_df = df_norm_g.reshape(depth, 1, -1)
    w_br = [w.astype(BF16) for w in (w_br_sb, w_br_df, w_br_cv)]
    w_o_b, w_up_b, w_down_b = w_o.astype(BF16), w_up.astype(BF16), w_down.astype(BF16)
    ln = [a.reshape(depth, 1, d) for a in (ln1_g, ln1_b, ln2_g, ln2_b)]
    hist_in = state_conv.reshape(depth, nseq, 2 * cv_w)

    all_projs = [[], []]
    hists = [[], []]
    for l in range(depth):
        lam_init = 0.8 - 0.6 * math.exp(-0.3 * l)
        projs = [_matmul(xbs[s], w_in, layer=l, tm=_tile(xs[s].shape[0], 1024), tn=_tile(w_in.shape[2], 1024), name="in_proj")
                 for s in range(2)]
        for s in range(2):
            all_projs[s].append(projs[s])

        o_sb_p = _sb_prompt(projs[0], u2, batch=batch, seq=seq, heads=sb_heads, dh=sb_dh,
                            q_col=sb_q_col, k_col=sb_k_col, v_col=sb_v_col)
        o_df_p = _df_prompt(projs[0], df_lambda, g_df, layer=l, batch=batch, seq=seq, heads=df_heads,
                            dh=df_dh, q_col=df_q_col, k_col=df_k_col, v_col=df_v_col, lam_init=lam_init)
        o_cv_p, hist_p = _conv_prompt(projs[0], conv_w, layer=l, batch=batch, seq=seq, width=cv_w,
                                      b_col=cv_col)

        proj_s = projs[1]
        qkv_new = jnp.pad(proj_s[:, :cv_col].reshape(nseq, steps, cv_col),
                          ((0, 0), (0, QPAD - steps), (0, 0)))
        o_sb_s, o_df_s = _decode_attn(qkv_new, page_table, cache_sb_k, cache_sb_v, cache_df_k,
                                      cache_df_v, u2, df_lambda, g_df,
                                      layer=l, cols=cols, lam_init=lam_init)
        o_sb_s = o_sb_s[:, :steps].reshape(ts, sb_w)
        o_df_s = o_df_s[:, :steps].reshape(ts, df_w)
        cvs = [proj_s[:, cv_col + i * cv_w:cv_col + (i + 1) * cv_w].reshape(nseq, steps * cv_w)
               for i in range(3)]
        o_cv_s, hist_s = _conv_sample(cvs[0], cvs[1], cvs[2], hist_in, conv_w, layer=l, width=cv_w,
                                      steps=steps)
        hists[0].append(hist_p)
        hists[1].append(hist_s.reshape(nseq, 2, cv_w))

        branches = [(o_sb_p, o_df_p, o_cv_p), (o_sb_s, o_df_s, o_cv_s.reshape(ts, cv_w))]
        for s in range(2):
            merged = _merge(*branches[s], projs[s], *w_br, layer=l, gate_col=gate_col,
                            tm=_tile(xs[s].shape[0], 256))
            xs[s], xbs[s] = _oproj_ln(merged, w_o_b, xs[s], ln[0], ln[1], layer=l, alpha=alpha, tm=tms[s])
            xs[s], xbs[s] = _mlp_ln(xbs[s], w_up_b, w_down_b, xs[s], ln[2], ln[3], layer=l,
                                    alpha=alpha, tm=tms[s], tf=_tile(w_up.shape[2], 1024))

    def cache_entries(flat, lead):
        sbk, sbv, dfk, dfv = flat
        dfv = dfv.reshape(depth, *lead, 2, df_heads, df_dh).swapaxes(-3, -2)
        return (sbk.reshape(depth, *lead, sb_heads, sb_dh), sbv.reshape(depth, *lead, sb_heads, sb_dh),
                dfk.reshape(depth, *lead, df_heads, 2, df_dh), dfv.reshape(depth, *lead, df_heads, 2 * df_dh))

    kv_flat = [_kv_out(all_projs[s], tm=tms[s], **kv_kw) for s in range(2)]
    return (xs[0].reshape(batch, seq, d), xs[1].reshape(nseq, steps, d),
            *cache_entries(kv_flat[0], (batch, seq)), jnp.stack(hists[0], 0),
            *cache_entries(kv_flat[1], (nseq, steps)), jnp.stack(hists[1], 0))
```

```python
import functools
import math

import jax
import jax.numpy as jnp
from jax import lax
from jax.experimental import pallas as pl
from jax.experimental.pallas import tpu as pltpu

F32 = jnp.float32
BF16 = jnp.bfloat16

LN_EPS = 1e-5
RMS_EPS = 1e-6
LANES = 128
SUBLANES = 8
VMEM_LIMIT = 60 * 1024 * 1024
Q_TILE = 256
SCAN_W = 128
QPAD = 8
SB_HEADS_PER_STEP = 4
DF_HEADS_PER_STEP = 4
CHUNK_PAGES = 4
N_SLOTS = 3


def _cparams(n_axes):
    return pltpu.CompilerParams(dimension_semantics=("arbitrary",) * n_axes,
                                vmem_limit_bytes=VMEM_LIMIT)


def _layer_spec(a, layer):
    tail = (0,) * (a.ndim - 1)
    return pl.BlockSpec((None,) + a.shape[1:], lambda *_: (layer,) + tail)


def _dot(a, b):
    return jnp.dot(a, b, preferred_element_type=F32)


def _dot_nt(a, b):
    return lax.dot_general(a, b, (((1,), (1,)), ((), ())), preferred_element_type=F32)


def _log_sigmoid(z):
    return jnp.minimum(z, 0.0) - jnp.log(1.0 + jnp.exp(-jnp.abs(z)))


def _layernorm(y, g, b):
    mu = jnp.mean(y, axis=-1, keepdims=True)
    yc = y - mu
    var = jnp.mean(yc * yc, axis=-1, keepdims=True)
    return yc * lax.rsqrt(var + LN_EPS) * g + b


def _scan_matrix(tk):
    r = jnp.arange(tk)[:, None]
    c = jnp.arange(tk)[None, :]
    upper = (r > c).astype(BF16)
    half = jnp.concatenate([upper, jnp.ones((tk, tk), BF16)], axis=1)
    return jnp.concatenate([half, half], axis=0)


def _mm_kernel(x_ref, w_ref, o_ref, wb_ref):
    @pl.when(pl.program_id(1) == 0)
    def _():
        wb_ref[...] = w_ref[...].astype(BF16)

    o_ref[...] = _dot(x_ref[...], wb_ref[...]).astype(o_ref.dtype)


def _matmul(x, w, *, layer, tm, tn, name):
    m, k = x.shape
    n = w.shape[2]
    return pl.pallas_call(
        _mm_kernel,
        grid=(n // tn, m // tm),
        in_specs=[pl.BlockSpec((tm, k), lambda j, i: (i, 0)),
                  pl.BlockSpec((None, k, tn), lambda j, i: (layer, 0, j))],
        out_specs=pl.BlockSpec((tm, tn), lambda j, i: (i, j)),
        out_shape=jax.ShapeDtypeStruct((m, n), F32),
        scratch_shapes=[pltpu.VMEM((k, tn), BF16)],
        compiler_params=_cparams(2),
        name=name,
    )(x, w)


def _sb_weights(z, u2, carry, valid):
    w = u2.shape[0] // 2
    nseg = z.shape[1] // w
    ls = _log_sigmoid(z)
    lk = ls - z
    if valid is not None:
        lk = jnp.where(valid, lk, 0.0)
    hi = lk.astype(BF16)
    lo = (lk - hi.astype(F32)).astype(BF16)
    after = [None] * nseg
    for s in reversed(range(nseg)):
        seg = slice(s * w, (s + 1) * w)
        sc = _dot(jnp.concatenate([hi[:, seg], lo[:, seg]], axis=1), u2)
        after[s] = sc[:, :w] + carry
        carry = carry + sc[:, w:]
    a = jnp.exp(ls + (after[0] if nseg == 1 else jnp.concatenate(after, axis=1)))
    if valid is not None:
        a = jnp.where(valid, a, 0.0)
    return a, carry


def _sb_prompt_kernel(q_ref, k_ref, v_ref, u2_ref, o_ref, carry_ref, acc_ref, *, scale, dh):
    tq = q_ref.shape[0]
    hp = q_ref.shape[1] // dh
    qi = pl.program_id(2)
    heads = [slice(h * dh, (h + 1) * dh) for h in range(hp)]
    qs = [q_ref[:, hs].astype(BF16) for hs in heads]
    u2 = u2_ref[...]

    def block(j, width, diagonal):
        off = j * tq if isinstance(j, int) else pl.multiple_of(j * tq, tq)
        valid = None
        if diagonal:
            valid = (lax.broadcasted_iota(jnp.int32, (tq, tq), 1)
                     < lax.broadcasted_iota(jnp.int32, (tq, tq), 0))
        for h, hs in enumerate(heads):
            kb = k_ref[pl.ds(off, width), hs].astype(BF16)
            vb = v_ref[pl.ds(off, width), hs].astype(BF16)
            carry = jnp.zeros((tq, SCAN_W), F32) if diagonal else carry_ref[h]
            a, carry = _sb_weights(_dot_nt(qs[h], kb) * scale, u2, carry, valid)
            pv = _dot(a.astype(BF16), vb)
            carry_ref[h] = carry
            acc_ref[h] = pv if diagonal else acc_ref[h] + pv

    block(qi, tq, True)

    def body(t, _):
        block(qi - 2 - 2 * t, 2 * tq, False)
        return 0

    lax.fori_loop(0, qi // 2, body, 0)

    @pl.when(qi % 2 == 1)
    def _():
        block(0, tq, False)

    for h, hs in enumerate(heads):
        o_ref[:, hs] = acc_ref[h].astype(o_ref.dtype)


def _sb_prompt(proj, u2, *, batch, seq, heads, dh, q_col, k_col, v_col):
    nq = seq // Q_TILE
    hp = math.gcd(heads, SB_HEADS_PER_STEP)
    w = hp * dh
    kern = functools.partial(_sb_prompt_kernel, scale=dh ** -0.5, dh=dh)
    return pl.pallas_call(
        kern,
        grid=(batch, heads // hp, nq),
        in_specs=[
            pl.BlockSpec((Q_TILE, w), lambda b, h, i: (b * nq + i, q_col // w + h)),
            pl.BlockSpec((seq, w), lambda b, h, i: (b, k_col // w + h)),
            pl.BlockSpec((seq, w), lambda b, h, i: (b, v_col // w + h)),
            pl.BlockSpec(u2.shape, lambda b, h, i: (0, 0)),
        ],
        out_specs=pl.BlockSpec((Q_TILE, w), lambda b, h, i: (b * nq + i, h)),
        out_shape=jax.ShapeDtypeStruct((batch * seq, heads * dh), BF16),
        scratch_shapes=[pltpu.VMEM((hp, Q_TILE, SCAN_W), F32),
                        pltpu.VMEM((hp, Q_TILE, dh), F32)],
        compiler_params=_cparams(3),
        name="sb_prompt",
    )(proj, proj, proj, u2)


def _alibi_slope(h, heads):
    s = jnp.float32(2.0 ** (-8.0 * heads / heads))
    for i in range(heads - 2, -1, -1):
        s = jnp.where(h == i, jnp.float32(2.0 ** (-8.0 * (i + 1) / heads)), s)
    return s


def _df_lambda(lam_ref, lam_init):
    l = lam_ref[...]
    s1 = jnp.sum(l[0:1] * l[1:2], axis=1, keepdims=True)
    s2 = jnp.sum(l[2:3] * l[3:4], axis=1, keepdims=True)
    return jnp.exp(s1) - jnp.exp(s2) + lam_init


def _df_finish(acc0, l0, acc1, l1, lam, g, lam_init):
    o = acc0 / l0 - lam * (acc1 / l1)
    o = o * lax.rsqrt(jnp.mean(o * o, axis=-1, keepdims=True) + RMS_EPS)
    return o * g * (1.0 - lam_init)


def _df_prompt_kernel(*refs, groups, scale, lam_init, heads, dh):
    q_refs, k_refs, v_refs = refs[:groups], refs[groups:2 * groups], refs[2 * groups:3 * groups]
    lam_ref, g_ref, o_ref, kb_ref, vb_ref, p_ref = refs[3 * groups:]
    tq, gw = q_refs[0].shape
    dv = 2 * dh
    hp = groups * gw // dv
    nq = k_refs[0].shape[0] // tq
    qi = pl.program_id(2)
    units = range(2 * hp)

    @pl.when(qi == 0)
    def _():
        for gi in range(groups):
            kb_ref[:, gi * gw:(gi + 1) * gw] = k_refs[gi][...].astype(BF16)
            vb_ref[:, gi * gw:(gi + 1) * gw] = v_refs[gi][...].astype(BF16)

    slopes = [_alibi_slope(pl.program_id(1) * hp + h, heads) for h in range(hp)]
    qs = [q_refs[n * dh // gw][:, n * dh % gw:n * dh % gw + dh].astype(BF16) for n in units]
    row = lax.broadcasted_iota(jnp.int32, (tq, tq), 0)
    col = lax.broadcasted_iota(jnp.int32, (tq, tq), 1)
    rel = (row - col).astype(F32)

    def scores(j, n, diagonal):
        off = pl.multiple_of(j * tq, tq)
        dist = rel + ((qi - j) * tq).astype(F32)
        s = _dot_nt(qs[n], kb_ref[pl.ds(off, tq), n * dh:(n + 1) * dh]) * scale - slopes[n // 2] * dist
        return jnp.where(rel >= 0, s, -jnp.inf) if diagonal else s

    def row_max(j, n, diagonal):
        return jnp.max(scores(j, n, diagonal), axis=1, keepdims=True)

    m = tuple(row_max(qi, n, True) for n in units)
    m = lax.fori_loop(0, qi, lambda j, m: tuple(jnp.maximum(m[n], row_max(j, n, False)) for n in units), m)

    def weights(j, n, diagonal):
        p = jnp.exp(scores(j, n, diagonal) - m[n])
        p_ref[n, j] = p.astype(BF16)
        return jnp.sum(p, axis=1, keepdims=True)

    l = tuple(weights(qi, n, True) for n in units)
    l = lax.fori_loop(0, qi, lambda j, l: tuple(l[n] + weights(j, n, False) for n in units), l)

    lam = _df_lambda(lam_ref, lam_init)
    g = g_ref[...]

    def finish(n_blocks):
        keys = n_blocks * tq
        for h in range(hp):
            vb = vb_ref[0:keys, h * dv:(h + 1) * dv]
            acc = [_dot(jnp.concatenate([p_ref[2 * h + c, jj] for jj in range(n_blocks)], axis=1), vb)
                   for c in range(2)]
            o = _df_finish(acc[0], l[2 * h], acc[1], l[2 * h + 1], lam, g, lam_init)
            o_ref[:, h * dv:(h + 1) * dv] = o.astype(o_ref.dtype)

    for k in range(nq):
        pl.when(qi == k)(functools.partial(finish, k + 1))


def _df_prompt(proj, lam_p, g, *, layer, batch, seq, heads, dh, q_col, k_col, v_col, lam_init):
    nq = seq // Q_TILE
    hp = math.gcd(heads, DF_HEADS_PER_STEP)
    w = hp * 2 * dh
    gw = math.gcd(math.gcd(w, q_col), math.gcd(k_col, v_col))
    groups = w // gw

    def col_blocks(rows, col, row_map):
        return [pl.BlockSpec((rows, gw), lambda b, h, i, n=n: (row_map(b, i), col // gw + h * groups + n))
                for n in range(groups)]

    kern = functools.partial(_df_prompt_kernel, groups=groups, scale=dh ** -0.5, lam_init=lam_init,
                             heads=heads, dh=dh)
    return pl.pallas_call(
        kern,
        grid=(batch, heads // hp, nq),
        in_specs=(col_blocks(Q_TILE, q_col, lambda b, i: b * nq + i)
                  + col_blocks(seq, k_col, lambda b, i: b)
                  + col_blocks(seq, v_col, lambda b, i: b)
                  + [_layer_spec(lam_p, layer), _layer_spec(g, layer)]),
        out_specs=pl.BlockSpec((Q_TILE, w), lambda b, h, i: (b * nq + i, h)),
        out_shape=jax.ShapeDtypeStruct((batch * seq, heads * 2 * dh), BF16),
        scratch_shapes=[pltpu.VMEM((seq, w), BF16),
                        pltpu.VMEM((seq, w), BF16),
                        pltpu.VMEM((2 * hp, nq, Q_TILE, Q_TILE), BF16)],
        compiler_params=_cparams(3),
        name="df_prompt",
    )(*([proj] * (3 * groups)), lam_p, g)


def _decode_kernel(pt_ref, qkv_ref, sbk_hbm, sbv_hbm, dfk_hbm, dfv_hbm, u2_ref, lam_ref, g_ref,
                   osb_ref, odf_ref,
                   sbk_buf, sbv_buf, dfk_buf, dfv_buf, sems, sb_carry, sb_acc, df_m, df_l, df_acc,
                   *, layer, n_seq, n_pages, page, sb_heads, sb_dh, df_heads, df_dh, cols, lam_init):
    b = pl.program_id(0)
    n_chunks = n_pages // CHUNK_PAGES
    total_chunks = n_seq * n_chunks
    past_len = n_pages * page
    sb_q_col, sb_k_col, sb_v_col, df_q_col, df_k_col, df_v_col = cols
    sb_scale = sb_dh ** -0.5
    df_scale = df_dh ** -0.5
    dv = 2 * df_dh

    def chunk_copies(c, slot):
        seq = c // n_chunks
        first_page = n_pages - CHUNK_PAGES * (c % n_chunks + 1)
        copies = []
        for j in range(CHUNK_PAGES):
            phys = pt_ref[seq * n_pages + first_page + j]
            copies.append(pltpu.make_async_copy(sbk_hbm.at[layer, phys], sbk_buf.at[slot, j], sems.at[slot, 0]))
            copies.append(pltpu.make_async_copy(sbv_hbm.at[layer, phys], sbv_buf.at[slot, j], sems.at[slot, 1]))
            copies.append(pltpu.make_async_copy(dfk_hbm.at[layer, phys], dfk_buf.at[slot, j], sems.at[slot, 2]))
            copies.append(pltpu.make_async_copy(dfv_hbm.at[layer, phys], dfv_buf.at[slot, j], sems.at[slot, 3]))
        return copies

    def qslice(col, width):
        return qkv_ref[:, col:col + width].astype(BF16)

    def pad_rows(x):
        return jnp.concatenate([x, jnp.zeros((page - QPAD, x.shape[1]), x.dtype)], axis=0)

    def sb_update(ks, vs, kpos0, first):
        z = jnp.concatenate(
            [_dot_nt(qslice(sb_q_col + h * sb_dh, sb_dh), ks[h]) for h in range(sb_heads)],
            axis=0) * sb_scale
        valid = None
        if first:
            qpos = past_len + (lax.broadcasted_iota(jnp.int32, z.shape, 0) & (QPAD - 1))
            valid = kpos0 + lax.broadcasted_iota(jnp.int32, z.shape, 1) < qpos
        carry = jnp.zeros((z.shape[0], SCAN_W), F32) if first else sb_carry[...]
        a, carry = _sb_weights(z, u2_ref[...], carry, valid)
        a = a.astype(BF16)
        pv = jnp.concatenate(
            [_dot(a[h * QPAD:(h + 1) * QPAD], vs[h]) for h in range(sb_heads)], axis=0)
        sb_carry[...] = carry
        sb_acc[...] = pv if first else sb_acc[...] + pv

    def df_update(ks, vs, kpos0, first):
        s = jnp.concatenate(
            [_dot_nt(qslice(df_q_col + (2 * h + c) * df_dh, df_dh), ks[h][c])
             for h in range(df_heads) for c in range(2)], axis=0) * df_scale
        ridx = lax.broadcasted_iota(jnp.int32, s.shape, 0)
        qpos = past_len + (ridx & (QPAD - 1))
        kpos = kpos0 + lax.broadcasted_iota(jnp.int32, s.shape, 1)
        dist = (qpos - kpos).astype(F32)
        slope = _alibi_slope(ridx // (2 * QPAD), df_heads)
        s = s - slope * dist
        if first:
            s = jnp.where(dist >= 0, s, -jnp.inf)
        m_blk = jnp.max(s, axis=1, keepdims=True)
        m_new = m_blk if first else jnp.maximum(df_m[...], m_blk)
        p = jnp.exp(s - m_new)
        psum = jnp.sum(p, axis=1, keepdims=True)
        p = p.astype(BF16)
        pv = jnp.concatenate(
            [_dot(p[h * 2 * QPAD:(h + 1) * 2 * QPAD], vs[h]) for h in range(df_heads)], axis=0)
        if first:
            df_l[...] = psum
            df_acc[...] = pv
        else:
            alpha = jnp.exp(df_m[...] - m_new)
            df_l[...] = alpha * df_l[...] + psum
            df_acc[...] = alpha * df_acc[...] + pv
        df_m[...] = m_new

    ahead = N_SLOTS - 1

    @pl.when(b == 0)
    def _prime():
        for c in range(min(ahead, total_chunks)):
            for cp in chunk_copies(c, c % N_SLOTS):
                cp.start()

    ks = [pad_rows(qslice(sb_k_col + h * sb_dh, sb_dh)) for h in range(sb_heads)]
    vs = [pad_rows(qslice(sb_v_col + h * sb_dh, sb_dh)) for h in range(sb_heads)]
    sb_update(ks, vs, past_len, True)
    kd = [[pad_rows(qslice(df_k_col + (2 * h + c) * df_dh, df_dh)) for c in range(2)]
          for h in range(df_heads)]
    vd = [pad_rows(qslice(df_v_col + h * dv, dv)) for h in range(df_heads)]
    df_update(kd, vd, past_len, True)

    def cached_chunk(i, _):
        c = b * n_chunks + i
        slot = c % N_SLOTS

        @pl.when(c + ahead < total_chunks)
        def _prefetch():
            for cp in chunk_copies(c + ahead, (c + ahead) % N_SLOTS):
                cp.start()

        for cp in chunk_copies(c, slot):
            cp.wait()

        def head_rows(buf, off, stride):
            return jnp.concatenate([buf[slot, j, pl.ds(off, page, stride=stride), :]
                                    for j in range(CHUNK_PAGES)], axis=0).astype(BF16)

        kpos0 = (n_pages - CHUNK_PAGES * (i + 1)) * page
        ks = [head_rows(sbk_buf, h, sb_heads) for h in range(sb_heads)]
        vs = [head_rows(sbv_buf, h, sb_heads) for h in range(sb_heads)]
        sb_update(ks, vs, kpos0, False)
        kd = [[head_rows(dfk_buf, 2 * h + c, 2 * df_heads) for c in range(2)] for h in range(df_heads)]
        vd = [jnp.concatenate([head_rows(dfv_buf, e * df_heads + h, 2 * df_heads) for e in range(2)], axis=1)
              for h in range(df_heads)]
        df_update(kd, vd, kpos0, False)
        return 0

    lax.fori_loop(0, n_chunks, cached_chunk, 0)

    for h in range(sb_heads):
        osb_ref[:, h * sb_dh:(h + 1) * sb_dh] = sb_acc[h * QPAD:(h + 1) * QPAD, :].astype(osb_ref.dtype)
    lam = _df_lambda(lam_ref, lam_init)
    g = g_ref[...]
    for h in range(df_heads):
        r0 = h * 2 * QPAD
        o = _df_finish(df_acc[r0:r0 + QPAD, :], df_l[r0:r0 + QPAD, :],
                       df_acc[r0 + QPAD:r0 + 2 * QPAD, :], df_l[r0 + QPAD:r0 + 2 * QPAD, :],
                       lam, g, lam_init)
        odf_ref[:, h * dv:(h + 1) * dv] = o.astype(odf_ref.dtype)


def _decode_attn(qkv_new, page_table, cache_sb_k, cache_sb_v, cache_df_k, cache_df_v, u2, lam_p, g,
                 *, layer, cols, lam_init):
    nseq, _, width = qkv_new.shape
    n_pages = page_table.shape[1]
    _, _, page, sb_heads, sb_dh = cache_sb_k.shape
    df_heads, df_dh = cache_df_k.shape[3], cache_df_k.shape[5]
    dv = 2 * df_dh
    depth, n_phys = cache_sb_k.shape[:2]
    assert n_pages % CHUNK_PAGES == 0 and page == SCAN_W

    kern = functools.partial(_decode_kernel, layer=layer, n_seq=nseq, n_pages=n_pages, page=page,
                             sb_heads=sb_heads, sb_dh=sb_dh, df_heads=df_heads, df_dh=df_dh,
                             cols=cols, lam_init=lam_init)
    hbm = pl.BlockSpec(memory_space=pl.ANY)
    grid_spec = pltpu.PrefetchScalarGridSpec(
        num_scalar_prefetch=1,
        grid=(nseq,),
        in_specs=[
            pl.BlockSpec((None, QPAD, width), lambda b, pt: (b, 0, 0)),
            hbm, hbm, hbm, hbm,
            pl.BlockSpec(u2.shape, lambda b, pt: (0, 0)),
            _layer_spec(lam_p, layer),
            _layer_spec(g, layer),
        ],
        out_specs=[
            pl.BlockSpec((None, QPAD, sb_heads * sb_dh), lambda b, pt: (b, 0, 0)),
            pl.BlockSpec((None, QPAD, df_heads * dv), lambda b, pt: (b, 0, 0)),
        ],
        scratch_shapes=[
            pltpu.VMEM((N_SLOTS, CHUNK_PAGES, page * sb_heads, sb_dh), F32),
            pltpu.VMEM((N_SLOTS, CHUNK_PAGES, page * sb_heads, sb_dh), F32),
            pltpu.VMEM((N_SLOTS, CHUNK_PAGES, page * df_heads * 2, df_dh), F32),
            pltpu.VMEM((N_SLOTS, CHUNK_PAGES, page * 2 * df_heads, df_dh), F32),
            pltpu.SemaphoreType.DMA((N_SLOTS, 4)),
            pltpu.VMEM((sb_heads * QPAD, SCAN_W), F32),
            pltpu.VMEM((sb_heads * QPAD, sb_dh), F32),
            pltpu.VMEM((df_heads * 2 * QPAD, 1), F32),
            pltpu.VMEM((df_heads * 2 * QPAD, 1), F32),
            pltpu.VMEM((df_heads * 2 * QPAD, dv), F32),
        ],
    )
    df_v_rows = (cache_df_v.reshape(depth, n_phys, page, df_heads, 2, df_dh).swapaxes(3, 4)
                 .reshape(depth, n_phys, page * 2 * df_heads, df_dh))
    return pl.pallas_call(
        kern,
        grid_spec=grid_spec,
        out_shape=[jax.ShapeDtypeStruct((nseq, QPAD, sb_heads * sb_dh), BF16),
                   jax.ShapeDtypeStruct((nseq, QPAD, df_heads * dv), BF16)],
        compiler_params=_cparams(1),
        name="decode_attn",
    )(page_table.reshape(-1), qkv_new,
      cache_sb_k.reshape(depth, n_phys, page * sb_heads, sb_dh),
      cache_sb_v.reshape(depth, n_phys, page * sb_heads, sb_dh),
      cache_df_k.reshape(depth, n_phys, page * df_heads * 2, df_dh),
      df_v_rows, u2, lam_p, g)


def _conv_prompt_kernel(b_ref, c_ref, h_ref, w_ref, o_ref, hist_ref):
    u = c_ref[...] * h_ref[...]
    s = u.shape[0]
    row = lax.broadcasted_iota(jnp.int32, u.shape, 0)
    u1 = jnp.where(row >= 1, pltpu.roll(u, 1, axis=0), 0.0)
    u2 = jnp.where(row >= 2, pltpu.roll(u, 2, axis=0), 0.0)
    w = w_ref[...]
    conv = w[0:1] * u2 + w[1:2] * u1 + w[2:3] * u
    o_ref[...] = (b_ref[...] * conv).astype(o_ref.dtype)
    hist_ref[...] = u[s - 2:s, :]


def _conv_prompt(proj, conv_w, *, layer, batch, seq, width, b_col):
    nc = width // LANES
    cb = b_col // LANES
    return pl.pallas_call(
        _conv_prompt_kernel,
        grid=(batch, nc),
        in_specs=[
            pl.BlockSpec((seq, LANES), lambda b, c: (b, cb + c)),
            pl.BlockSpec((seq, LANES), lambda b, c: (b, cb + nc + c)),
            pl.BlockSpec((seq, LANES), lambda b, c: (b, cb + 2 * nc + c)),
            pl.BlockSpec((None, conv_w.shape[1], LANES), lambda b, c: (layer, 0, c)),
        ],
        out_specs=[pl.BlockSpec((seq, LANES), lambda b, c: (b, c)),
                   pl.BlockSpec((None, 2, LANES), lambda b, c: (b, 0, c))],
        out_shape=[jax.ShapeDtypeStruct((batch * seq, width), BF16),
                   jax.ShapeDtypeStruct((batch, 2, width), F32)],
        compiler_params=_cparams(2),
        name="conv_prompt",
    )(proj, proj, proj, conv_w)


def _conv_sample_kernel(b_ref, c_ref, h_ref, hist_ref, w_ref, o_ref, nh_ref, *, width, steps):
    u = c_ref[...] * h_ref[...]
    ext = jnp.concatenate([hist_ref[...], u], axis=1)
    w = w_ref[...]
    b = b_ref[...]
    for q in range(steps):
        conv = (w[0:1] * ext[:, q * width:(q + 1) * width]
                + w[1:2] * ext[:, (q + 1) * width:(q + 2) * width]
                + w[2:3] * ext[:, (q + 2) * width:(q + 3) * width])
        o_ref[:, q * width:(q + 1) * width] = (b[:, q * width:(q + 1) * width] * conv).astype(o_ref.dtype)
    nh_ref[...] = ext[:, steps * width:(steps + 2) * width]


def _conv_sample(cv_b, cv_c, cv_h, hist, conv_w, *, layer, width, steps):
    nseq = cv_b.shape[0]
    kern = functools.partial(_conv_sample_kernel, width=width, steps=steps)
    full = lambda a: pl.BlockSpec(a.shape, lambda i: (0,) * a.ndim)
    return pl.pallas_call(
        kern,
        grid=(1,),
        in_specs=[full(cv_b), full(cv_c), full(cv_h), _layer_spec(hist, layer), _layer_spec(conv_w, layer)],
        out_specs=[pl.BlockSpec((nseq, steps * width), lambda i: (0, 0)),
                   pl.BlockSpec((nseq, 2 * width), lambda i: (0, 0))],
        out_shape=[jax.ShapeDtypeStruct((nseq, steps * width), BF16),
                   jax.ShapeDtypeStruct((nseq, 2 * width), F32)],
        compiler_params=_cparams(1),
        name="conv_sample",
    )(cv_b, cv_c, cv_h, hist, conv_w)


def _merge_oproj_kernel(osb_ref, odf_ref, ocv_ref, gsb_ref, gdf_ref, gcv_ref, wsb_ref, wdf_ref, wcv_ref,
                        wo_ref, x_ref, g_ref, b_ref, xo_ref, xb_ref, *, alpha):
    m = (jax.nn.sigmoid(gsb_ref[...]) * _dot(osb_ref[...], wsb_ref[...])
         + jax.nn.sigmoid(gdf_ref[...]) * _dot(odf_ref[...], wdf_ref[...])
         + jax.nn.sigmoid(gcv_ref[...]) * _dot(ocv_ref[...], wcv_ref[...]))
    y = alpha * x_ref[...] + _dot(m.astype(BF16), wo_ref[...])
    out = _layernorm(y, g_ref[...], b_ref[...])
    xo_ref[...] = out
    xb_ref[...] = out.astype(xb_ref.dtype)


def _merge_oproj_ln(o_sb, o_df, o_cv, proj, w_sb, w_df, w_cv, w_o, x, g, b, *, layer, gate_col, alpha, tm):
    t, d = x.shape
    gb = gate_col // d
    row = lambda w: pl.BlockSpec((tm, w), lambda i: (i, 0))
    res = lambda a: pl.BlockSpec((None,) + a.shape[1:], lambda i: (layer, 0, 0), pipeline_mode=pl.Buffered(1))
    return pl.pallas_call(
        functools.partial(_merge_oproj_kernel, alpha=alpha),
        grid=(t // tm,),
        in_specs=[row(o_sb.shape[1]), row(o_df.shape[1]), row(o_cv.shape[1]),
                  pl.BlockSpec((tm, d), lambda i: (i, gb)),
                  pl.BlockSpec((tm, d), lambda i: (i, gb + 1)),
                  pl.BlockSpec((tm, d), lambda i: (i, gb + 2)),
                  res(w_sb), res(w_df), res(w_cv), res(w_o), row(d), res(g), res(b)],
        out_specs=[row(d), row(d)],
        out_shape=[jax.ShapeDtypeStruct((t, d), F32), jax.ShapeDtypeStruct((t, d), BF16)],
        compiler_params=_cparams(1),
        name="merge_oproj_ln",
    )(o_sb, o_df, o_cv, proj, proj, proj, w_sb, w_df, w_cv, w_o, x, g, b)


def _mlp_kernel(xb_ref, wu_ref, wd_ref, x_ref, g_ref, b_ref, xo_ref, xbo_ref, acc_ref, *, alpha):
    f = pl.program_id(1)

    @pl.when(f == 0)
    def _():
        acc_ref[...] = jnp.zeros_like(acc_ref)

    h = jnp.maximum(_dot(xb_ref[...], wu_ref[...]), 0.0)
    acc_ref[...] += _dot((h * h).astype(BF16), wd_ref[...])

    @pl.when(f == pl.num_programs(1) - 1)
    def _():
        out = _layernorm(alpha * x_ref[...] + acc_ref[...], g_ref[...], b_ref[...])
        xo_ref[...] = out
        xbo_ref[...] = out.astype(xbo_ref.dtype)


def _mlp_ln(xb, w_up, w_down, x, g, b, *, layer, alpha, tm, tf):
    t, d = x.shape
    ff = w_up.shape[2]
    row = pl.BlockSpec((tm, d), lambda i, f: (i, 0))
    return pl.pallas_call(
        functools.partial(_mlp_kernel, alpha=alpha),
        grid=(t // tm, ff // tf),
        in_specs=[row,
                  pl.BlockSpec((None, d, tf), lambda i, f: (layer, 0, f)),
                  pl.BlockSpec((None, tf, d), lambda i, f: (layer, f, 0)),
                  row, _layer_spec(g, layer), _layer_spec(b, layer)],
        out_specs=[row, row],
        out_shape=[jax.ShapeDtypeStruct((t, d), F32), jax.ShapeDtypeStruct((t, d), BF16)],
        scratch_shapes=[pltpu.VMEM((tm, d), F32)],
        compiler_params=_cparams(2),
        name="mlp_ln",
    )(xb, w_up, w_down, x, g, b)


def _kv_out_kernel(*refs, depth, sb_heads, sb_dh, df_heads, df_dh, n_df_blocks):
    per_layer = 2 + 2 * n_df_blocks
    o_sbk, o_sbv, o_dfk, o_dfv = refs[depth * per_layer:]
    tm, blk = refs[0].shape

    def emit(layer_refs):
        sbk_ref, sbv_ref = layer_refs[0], layer_refs[1]
        dfk_refs = layer_refs[2:2 + n_df_blocks]
        dfv_refs = layer_refs[2 + n_df_blocks:]

        def col(block_refs, c0, width):
            return block_refs[c0 // blk][:, c0 % blk:c0 % blk + width]

        for h in range(sb_heads):
            o_sbk[pl.ds(h, tm, stride=sb_heads), :] = sbk_ref[:, h * sb_dh:(h + 1) * sb_dh]
            o_sbv[pl.ds(h, tm, stride=sb_heads), :] = sbv_ref[:, h * sb_dh:(h + 1) * sb_dh]
        slots = 2 * df_heads
        for h in range(df_heads):
            for c in range(2):
                src = (2 * h + c) * df_dh
                o_dfk[pl.ds(2 * h + c, tm, stride=slots), :] = col(dfk_refs, src, df_dh)
                o_dfv[pl.ds(c * df_heads + h, tm, stride=slots), :] = col(dfv_refs, src, df_dh)

    for l in range(depth):
        pl.when(pl.program_id(0) == l)(
            functools.partial(emit, refs[l * per_layer:(l + 1) * per_layer]))


def _kv_out(projs, *, sb_heads, sb_dh, df_heads, df_dh, k_cols, tm):
    depth = len(projs)
    t = projs[0].shape[0]
    nt = t // tm
    sb_w = sb_heads * sb_dh
    df_w = df_heads * 2 * df_dh
    sb_k_col, sb_v_col, df_k_col, df_v_col = k_cols
    n_df_blocks = df_w // sb_w
    col0s = ([sb_k_col, sb_v_col] + [df_k_col + n * sb_w for n in range(n_df_blocks)]
             + [df_v_col + n * sb_w for n in range(n_df_blocks)])

    def blk(layer, c0):
        def index_map(l, i):
            return (jnp.where(l == layer, i, jnp.where(l < layer, 0, nt - 1)), c0 // sb_w)
        return pl.BlockSpec((tm, sb_w), index_map)

    in_specs = [blk(l, c0) for l in range(depth) for c0 in col0s]
    args = [p for p in projs for _ in col0s]
    rows = (sb_heads, sb_heads, 2 * df_heads, 2 * df_heads)
    widths = (sb_dh, sb_dh, df_dh, df_dh)
    kern = functools.partial(_kv_out_kernel, depth=depth, sb_heads=sb_heads, sb_dh=sb_dh,
                             df_heads=df_heads, df_dh=df_dh, n_df_blocks=n_df_blocks)
    return pl.pallas_call(
        kern,
        grid=(depth, nt),
        in_specs=in_specs,
        out_specs=[pl.BlockSpec((None, tm * r, w), lambda l, i: (l, i, 0)) for r, w in zip(rows, widths)],
        out_shape=[jax.ShapeDtypeStruct((depth, t * r, w), F32) for r, w in zip(rows, widths)],
        compiler_params=_cparams(2),
        name="kv_out",
    )(*args)


def _tile(n, pref):
    t = min(n, pref)
    while n % t:
        t //= 2
    return t


def kernel(x_prompt, x_sample, cache_sb_k, cache_sb_v, cache_df_k, cache_df_v, state_conv, page_table,
           w_in, conv_w, df_lambda, df_norm_g, w_br_sb, w_br_df, w_br_cv, w_o, ln1_g, ln1_b,
           w_up, w_down, ln2_g, ln2_b):
    batch, seq, d = x_prompt.shape
    nseq, steps, _ = x_sample.shape
    depth = w_in.shape[0]
    sb_heads, sb_dh = cache_sb_k.shape[3], cache_sb_k.shape[4]
    df_heads, df_dh = cache_df_k.shape[3], cache_df_k.shape[5]
    cv_w = state_conv.shape[3]
    sb_w = sb_heads * sb_dh
    df_w = df_heads * 2 * df_dh
    tp = batch * seq
    ts = nseq * steps
    alpha = (2 * depth) ** 0.25

    sb_q_col, sb_k_col, sb_v_col = 0, sb_w, 2 * sb_w
    df_q_col = 3 * sb_w
    df_k_col, df_v_col = df_q_col + df_w, df_q_col + 2 * df_w
    cv_col = df_q_col + 3 * df_w
    gate_col = cv_col + 3 * cv_w
    cols = (sb_q_col, sb_k_col, sb_v_col, df_q_col, df_k_col, df_v_col)

    u2 = _scan_matrix(SCAN_W)
    xs = [x_prompt.reshape(tp, d), x_sample.reshape(ts, d)]
    xbs = [x.astype(BF16) for x in xs]
    tms = [_tile(tp, 512), _tile(ts, 512)]
    kv_kw = dict(sb_heads=sb_heads, sb_dh=sb_dh, df_heads=df_heads, df_dh=df_dh,
                 k_cols=(sb_k_col, sb_v_col, df_k_col, df_v_col))

    g_df = df_norm_g.reshape(depth, 1, -1)
    w_br = [w.astype(BF16) for w in (w_br_sb, w_br_df, w_br_cv)]
    w_o_b, w_up_b, w_down_b = w_o.astype(BF16), w_up.astype(BF16), w_down.astype(BF16)
    ln = [a.reshape(depth, 1, d) for a in (ln1_g, ln1_b, ln2_g, ln2_b)]
    hist_in = state_conv.reshape(depth, nseq, 2 * cv_w)

    all_projs = [[], []]
    hists = [[], []]
    for l in range(depth):
        lam_init = 0.8 - 0.6 * math.exp(-0.3 * l)
        projs = [_matmul(xbs[s], w_in, layer=l, tm=_tile(xs[s].shape[0], 1024), tn=_tile(w_in.shape[2], 1024), name="in_proj")
                 for s in range(2)]
        for s in range(2):
            all_projs[s].append(projs[s])

        o_sb_p = _sb_prompt(projs[0], u2, batch=batch, seq=seq, heads=sb_heads, dh=sb_dh,
                            q_col=sb_q_col, k_col=sb_k_col, v_col=sb_v_col)
        o_df_p = _df_prompt(projs[0], df_lambda, g_df, layer=l, batch=batch, seq=seq, heads=df_heads,
                            dh=df_dh, q_col=df_q_col, k_col=df_k_col, v_col=df_v_col, lam_init=lam_init)
        o_cv_p, hist_p = _conv_prompt(projs[0], conv_w, layer=l, batch=batch, seq=seq, width=cv_w,
                                      b_col=cv_col)

        proj_s = projs[1]
        qkv_new = jnp.pad(proj_s[:, :cv_col].reshape(nseq, steps, cv_col),
                          ((0, 0), (0, QPAD - steps), (0, 0)))
        o_sb_s, o_df_s = _decode_attn(qkv_new, page_table, cache_sb_k, cache_sb_v, cache_df_k,
                                      cache_df_v, u2, df_lambda, g_df,
                                      layer=l, cols=cols, lam_init=lam_init)
        o_sb_s = o_sb_s[:, :steps].reshape(ts, sb_w)
        o_df_s = o_df_s[:, :steps].reshape(ts, df_w)
        cvs = [proj_s[:, cv_col + i * cv_w:cv_col + (i + 1) * cv_w].reshape(nseq, steps * cv_w)
               for i in range(3)]
        o_cv_s, hist_s = _conv_sample(cvs[0], cvs[1], cvs[2], hist_in, conv_w, layer=l, width=cv_w,
                                      steps=steps)
        hists[0].append(hist_p)
        hists[1].append(hist_s.reshape(nseq, 2, cv_w))

        branches = [(o_sb_p, o_df_p, o_cv_p), (o_sb_s, o_df_s, o_cv_s.reshape(ts, cv_w))]
        for s in range(2):
            xs[s], xbs[s] = _merge_oproj_ln(*branches[s], projs[s], *w_br, w_o_b, xs[s], ln[0], ln[1],
                                            layer=l, gate_col=gate_col, alpha=alpha,
                                            tm=_tile(xs[s].shape[0], 256))
            xs[s], xbs[s] = _mlp_ln(xbs[s], w_up_b, w_down_b, xs[s], ln[2], ln[3], layer=l,
                                    alpha=alpha, tm=tms[s], tf=_tile(w_up.shape[2], 1024))

    def cache_entries(flat, lead):
        sbk, sbv, dfk, dfv = flat
        dfv = dfv.reshape(depth, *lead, 2, df_heads, df_dh).swapaxes(-3, -2)
        return (sbk.reshape(depth, *lead, sb_heads, sb_dh), sbv.reshape(depth, *lead, sb_heads, sb_dh),
                dfk.reshape(depth, *lead, df_heads, 2, df_dh), dfv.reshape(depth, *lead, df_heads, 2 * df_dh))

    kv_flat = [_kv_out(all_projs[s], tm=tms[s], **kv_kw) for s in range(2)]
    return (xs[0].reshape(batch, seq, d), xs[1].reshape(nseq, steps, d),
            *cache_entries(kv_flat[0], (batch, seq)), jnp.stack(hists[0], 0),
            *cache_entries(kv_flat[1], (nseq, steps)), jnp.stack(hists[1], 0))
```
